```python
import jax
import jax.numpy as jnp
from jax import lax
import numpy as np

D_MODEL = 1024
BATCH = 32
SEQ = 2048
DEPTH = 1
DEC_BATCH = 128
DEC_SEQ = 8
PAST_LEN = 8192
PAGE_SIZE = 128

N_MEM = 256
MOBA_HEADS = 8
MOBA_HEAD_DIM = 64
MOBA_WIDTH = MOBA_HEADS * MOBA_HEAD_DIM
MOBA_BLOCK = 256
MOBA_TOPK = 3
MOBA_Q_CHUNK = 16
MEM_HEADS = 4
MEM_HEAD_DIM = 128
MEM_WIDTH = MEM_HEADS * MEM_HEAD_DIM
CONV_WIDTH = D_MODEL // 2
CONV_KERNEL = 31
N_BRANCH = 3
OFF_CONV = 0
OFF_Q = OFF_CONV + 2 * CONV_WIDTH
OFF_K = OFF_Q + MOBA_WIDTH
OFF_V = OFF_K + MOBA_WIDTH
OFF_QM = OFF_V + MOBA_WIDTH
OFF_GATE = OFF_QM + MEM_WIDTH
IN_WIDTH = OFF_GATE + N_BRANCH * D_MODEL
N_EXPERTS = 32
TOP_K = 4
D_FF = D_MODEL
SWIGLU_LIMIT = 7.0
SWIGLU_ALPHA = 1.702
MOE_BLOCK = 256
EPS = 1e-6

kernel_name = 'hybrid_moba_conformer_memory_moe_step'


def rms_norm(x, g):
    xf = x.astype(jnp.float32)
    y = xf * lax.rsqrt(jnp.mean(xf * xf, axis=-1, keepdims=True) + EPS)
    return (y * g.astype(jnp.float32)).astype(x.dtype)


def layer_norm(x, g, b):
    xf = x.astype(jnp.float32)
    mu = jnp.mean(xf, axis=-1, keepdims=True)
    var = jnp.mean(jnp.square(xf - mu), axis=-1, keepdims=True)
    y = (xf - mu) * lax.rsqrt(var + EPS)
    return (y * g.astype(jnp.float32) + b.astype(jnp.float32)).astype(x.dtype)


def project(x, g_norm1, w_in, g_q_moba, g_k_moba, g_q_mem):
    b, l, _ = x.shape
    z = rms_norm(x, g_norm1) @ w_in
    z_conv = z[..., OFF_CONV:OFF_Q]
    hshape = (b, l, MOBA_HEADS, MOBA_HEAD_DIM)
    q = rms_norm(z[..., OFF_Q:OFF_K].reshape(hshape), g_q_moba).transpose(0, 2, 1, 3)
    k = rms_norm(z[..., OFF_K:OFF_V].reshape(hshape), g_k_moba).transpose(0, 2, 1, 3)
    v = z[..., OFF_V:OFF_QM].reshape(hshape).transpose(0, 2, 1, 3)
    qm = rms_norm(z[..., OFF_QM:OFF_GATE].reshape(b, l, MEM_HEADS, MEM_HEAD_DIM), g_q_mem)
    gates = jax.nn.sigmoid(z[..., OFF_GATE:].reshape(b, l, N_BRANCH, D_MODEL))
    return z_conv, q, k, v, qm, gates


def conv_branch(z_conv, hist, w_dw, b_dw, g_ln, b_ln, w_conv_out, b_conv_out):
    u = z_conv[..., :CONV_WIDTH] * jax.nn.sigmoid(z_conv[..., CONV_WIDTH:])
    xp = jnp.concatenate([hist.astype(u.dtype), u], axis=1)
    y = lax.conv_general_dilated(xp, w_dw[:, None, :].astype(xp.dtype), (1,), 'VALID',
                                 dimension_numbers=('NWC', 'WIO', 'NWC'),
                                 feature_group_count=CONV_WIDTH) + b_dw
    y = jax.nn.silu(layer_norm(y, g_ln, b_ln))
    return y @ w_conv_out + b_conv_out, xp[:, -(CONV_KERNEL - 1):]


def mem_kv(mem, g_mem_norm, w_mem_kv, g_k_mem):
    b, m, _ = mem.shape
    kv = rms_norm(mem, g_mem_norm) @ w_mem_kv
    k = rms_norm(kv[..., :MEM_WIDTH].reshape(b, m, MEM_HEADS, MEM_HEAD_DIM), g_k_mem)
    v = kv[..., MEM_WIDTH:].reshape(b, m, MEM_HEADS, MEM_HEAD_DIM)
    return k, v


def mem_attend(qm, k, v):
    b, l = qm.shape[:2]
    s = jnp.einsum('blhd,bmhd->bhlm', qm, k).astype(jnp.float32) * (MEM_HEAD_DIM ** -0.5)
    p = jax.nn.softmax(s, axis=-1).astype(v.dtype)
    return jnp.einsum('bhlm,bmhd->blhd', p, v).reshape(b, l, MEM_WIDTH)


def moba_prompt(q, k, v):
    b, h, s, d = q.shape
    nb = -(-s // MOBA_BLOCK)
    pad = ((0, 0), (0, 0), (0, nb * MOBA_BLOCK - s), (0, 0))
    kb = jnp.pad(k, pad).reshape(b, h, nb, MOBA_BLOCK, d)
    vb = jnp.pad(v, pad).reshape(b, h, nb, MOBA_BLOCK, d)
    k_mean = jnp.mean(kb.astype(jnp.float32), axis=3)
    own = jnp.arange(s) // MOBA_BLOCK
    past = jnp.arange(nb)[None, :] < own[:, None]
    gate = jnp.einsum('bhsd,bhnd->bhsn', q.astype(jnp.float32), k_mean)
    gate = jnp.where(past, gate, -jnp.inf)
    n_sel = min(MOBA_TOPK, nb)
    _, sel = lax.top_k(gate, n_sel)
    sel_ok = sel < own[:, None]
    bi = jnp.arange(b)[:, None, None, None]
    hi = jnp.arange(h)[None, :, None, None]
    scale = MOBA_HEAD_DIM ** -0.5
    n_s = n_sel * MOBA_BLOCK

    def chunk(c):
        t0 = c * MOBA_Q_CHUNK
        qc = lax.dynamic_slice_in_dim(q, t0, MOBA_Q_CHUNK, axis=2)
        idx = lax.dynamic_slice_in_dim(sel, t0, MOBA_Q_CHUNK, axis=2)
        ok = lax.dynamic_slice_in_dim(sel_ok, t0, MOBA_Q_CHUNK, axis=2)
        k_sel = kb[bi, hi, idx].reshape(b, h, MOBA_Q_CHUNK, n_s, d)
        v_sel = vb[bi, hi, idx].reshape(b, h, MOBA_Q_CHUNK, n_s, d)
        blk = t0 // MOBA_BLOCK
        k_own = lax.dynamic_index_in_dim(kb, blk, axis=2, keepdims=False)
        v_own = lax.dynamic_index_in_dim(vb, blk, axis=2, keepdims=False)
        s_sel = jnp.einsum('bhqd,bhqkd->bhqk', qc, k_sel).astype(jnp.float32) * scale
        s_sel = jnp.where(jnp.repeat(ok, MOBA_BLOCK, axis=-1), s_sel, -jnp.inf)
        q_pos = t0 + jnp.arange(MOBA_Q_CHUNK)
        k_pos = blk * MOBA_BLOCK + jnp.arange(MOBA_BLOCK)
        s_own = jnp.einsum('bhqd,bhkd->bhqk', qc, k_own).astype(jnp.float32) * scale
        s_own = jnp.where(k_pos[None, :] <= q_pos[:, None], s_own, -jnp.inf)
        p = jax.nn.softmax(jnp.concatenate([s_sel, s_own], axis=-1), axis=-1).astype(v.dtype)
        return (jnp.einsum('bhqk,bhqkd->bhqd', p[..., :n_s], v_sel)
                + jnp.einsum('bhqk,bhkd->bhqd', p[..., n_s:], v_own))

    out = lax.map(chunk, jnp.arange(s // MOBA_Q_CHUNK))
    return out.transpose(1, 2, 0, 3, 4).reshape(b, h, s, d)


def moba_sample(q, k, v, cache_k, cache_v, page_table, layer):
    b, h, t, d = q.shape
    n_pages = page_table.shape[1]
    page = cache_k.shape[3]
    past_len = n_pages * page
    ppb = MOBA_BLOCK // page
    n_full = past_len // MOBA_BLOCK
    n_own_pages = (past_len - n_full * MOBA_BLOCK) // page
    scale = MOBA_HEAD_DIM ** -0.5
    s_parts, v_parts = [], []
    if n_full > 0:
        pt_full = page_table[:, :n_full * ppb]
        kf = cache_k[layer, pt_full].astype(jnp.float32)
        k_mean = jnp.mean(kf.reshape(b, n_full, ppb, h, page, d), axis=(2, 4))
        gate = jnp.einsum('bhtd,bnhd->bhtn', q.astype(jnp.float32), k_mean)
        n_sel = min(MOBA_TOPK, n_full)
        _, sel = lax.top_k(gate, n_sel)
        cols = sel[..., None] * ppb + jnp.arange(ppb)
        phys = page_table[jnp.arange(b)[:, None, None, None, None], cols]
        hi = jnp.arange(h)[None, :, None, None, None]
        k_sel = cache_k[layer, phys, hi].reshape(b, h, t, n_sel * MOBA_BLOCK, d)
        v_sel = cache_v[layer, phys, hi].reshape(b, h, t, n_sel * MOBA_BLOCK, d)
        s_parts.append(jnp.einsum('bhtd,bhtkd->bhtk', q, k_sel.astype(q.dtype)))
        v_parts.append(v_sel.astype(v.dtype))
    if n_own_pages > 0:
        pt_own = page_table[:, n_full * ppb:n_full * ppb + n_own_pages]
        k_oc = cache_k[layer, pt_own].transpose(0, 2, 1, 3, 4).reshape(b, h, n_own_pages * page, d)
        v_oc = cache_v[layer, pt_own].transpose(0, 2, 1, 3, 4).reshape(b, h, n_own_pages * page, d)
        s_parts.append(jnp.einsum('bhtd,bhkd->bhtk', q, k_oc.astype(q.dtype)))
        v_parts.append(v_oc.astype(v.dtype))
    causal = jnp.tril(jnp.ones((t, t), dtype=bool))
    s_new = jnp.einsum('bhtd,bhkd->bhtk', q, k).astype(jnp.float32) * scale
    s_new = jnp.where(causal, s_new, -jnp.inf)
    s_all = jnp.concatenate([sp.astype(jnp.float32) * scale for sp in s_parts] + [s_new], axis=-1)
    p = jax.nn.softmax(s_all, axis=-1).astype(v.dtype)
    o = jnp.einsum('bhtk,bhkd->bhtd', p[..., s_all.shape[-1] - t:], v)
    off = 0
    for sp, vp in zip(s_parts, v_parts):
        n = sp.shape[-1]
        eq = 'bhtk,bhtkd->bhtd' if vp.ndim == 5 else 'bhtk,bhkd->bhtd'
        o = o + jnp.einsum(eq, p[..., off:off + n], vp)
        off += n
    return o


def moe_ffn(h, w_router, b_router, w_gu, b_gu, w_down, b_down):
    lead = h.shape[:-1]
    hf = h.reshape(-1, D_MODEL)
    n = hf.shape[0]
    logits = (hf @ w_router + b_router).astype(jnp.float32)
    top_logit, top_e = lax.top_k(logits, TOP_K)
    gate = jax.nn.softmax(top_logit, axis=-1).astype(h.dtype)
    nk = n * TOP_K
    flat_e = top_e.reshape(nk)
    order = jnp.argsort(flat_e)
    e_sorted = flat_e[order]
    tok_sorted = (order // TOP_K).astype(jnp.int32)
    gate_sorted = gate.reshape(nk)[order]
    counts = jnp.bincount(flat_e, length=N_EXPERTS)
    padded = (counts + MOE_BLOCK - 1) // MOE_BLOCK * MOE_BLOCK
    pad_end = jnp.cumsum(padded)
    pad_start = pad_end - padded
    grp_start = jnp.cumsum(counts) - counts
    dest = pad_start[e_sorted] + jnp.arange(nk) - grp_start[e_sorted]
    n_blocks = (nk + N_EXPERTS * (MOE_BLOCK - 1) + MOE_BLOCK - 1) // MOE_BLOCK
    row_tok = jnp.full((n_blocks * MOE_BLOCK,), n, jnp.int32).at[dest].set(tok_sorted)
    block_e = jnp.minimum(jnp.searchsorted(pad_end, jnp.arange(n_blocks) * MOE_BLOCK, side='right'),
                          N_EXPERTS - 1)
    h_pad = jnp.concatenate([hf, jnp.zeros((1, D_MODEL), hf.dtype)], axis=0)
    xb = h_pad[row_tok].reshape(n_blocks, MOE_BLOCK, D_MODEL)

    def expert_block(args):
        xblk, e = args
        gu = xblk @ w_gu[e] + b_gu[e]
        g = jnp.minimum(gu[..., :D_FF], SWIGLU_LIMIT)
        u = jnp.clip(gu[..., D_FF:], -SWIGLU_LIMIT, SWIGLU_LIMIT)
        act = g * jax.nn.sigmoid(SWIGLU_ALPHA * g) * (u + 1.0)
        return act @ w_down[e] + b_down[e]

    yb = lax.map(expert_block, (xb, block_e)).reshape(n_blocks * MOE_BLOCK, D_MODEL)
    y = jnp.zeros((n, D_MODEL), h.dtype).at[tok_sorted].add(yb[dest] * gate_sorted[:, None])
    return y.reshape(lead + (D_MODEL,))


def merge_and_ffn(x, u_conv, o_moba, o_mem, gates, w_moba_o, w_mem_o, w_out, g_norm2,
                  w_router, b_router, w_gu, b_gu, w_down, b_down):
    b, l, _ = x.shape
    u_moba = o_moba.transpose(0, 2, 1, 3).reshape(b, l, MOBA_WIDTH) @ w_moba_o
    u_mem = o_mem @ w_mem_o
    merged = gates[:, :, 0] * u_conv + gates[:, :, 1] * u_moba + gates[:, :, 2] * u_mem
    x1 = x + merged @ w_out
    return x1 + moe_ffn(rms_norm(x1, g_norm2), w_router, b_router, w_gu, b_gu, w_down, b_down)


def setup_inputs(seed: int = 0) -> dict:
    key = jax.random.key(seed)
    keys = iter(jax.random.split(key, 48))

    def nrm(shape, scale):
        return jax.random.normal(next(keys), shape, jnp.float32) * scale

    def gain(shape):
        return 1.0 + nrm(shape, 0.02)

    n_pages = PAST_LEN // PAGE_SIZE
    n_phys = (DEC_BATCH * n_pages * 5 + 3) // 4
    L = DEPTH
    page_table = jax.random.permutation(next(keys), n_phys)[:DEC_BATCH * n_pages]
    page_table = page_table.reshape(DEC_BATCH, n_pages).astype(jnp.int32)
    return {
        'x_prompt': nrm((BATCH, SEQ, D_MODEL), 1.0),
        'x_sample': nrm((DEC_BATCH, DEC_SEQ, D_MODEL), 1.0),
        'mem_prompt': nrm((BATCH, N_MEM, D_MODEL), 1.0),
        'cache_moba_k': nrm((L, n_phys, MOBA_HEADS, PAGE_SIZE, MOBA_HEAD_DIM), 1.0),
        'cache_moba_v': nrm((L, n_phys, MOBA_HEADS, PAGE_SIZE, MOBA_HEAD_DIM), 1.0),
        'cache_mem_k': nrm((L, DEC_BATCH, N_MEM, MEM_HEADS, MEM_HEAD_DIM), 1.0),
        'cache_mem_v': nrm((L, DEC_BATCH, N_MEM, MEM_HEADS, MEM_HEAD_DIM), 1.0),
        'state_conv': nrm((L, DEC_BATCH, CONV_KERNEL - 1, CONV_WIDTH), 0.5),
        'page_table': page_table,
        'g_norm1': gain((L, D_MODEL)),
        'w_in': nrm((L, D_MODEL, IN_WIDTH), D_MODEL ** -0.5),
        'w_dw': nrm((L, CONV_KERNEL, CONV_WIDTH), CONV_KERNEL ** -0.5),
        'b_dw': nrm((L, CONV_WIDTH), 0.02),
        'g_conv_ln': gain((L, CONV_WIDTH)),
        'b_conv_ln': nrm((L, CONV_WIDTH), 0.02),
        'w_conv_out': nrm((L, CONV_WIDTH, D_MODEL), CONV_WIDTH ** -0.5),
        'b_conv_out': nrm((L, D_MODEL), 0.02),
        'g_q_moba': gain((L, MOBA_HEAD_DIM)),
        'g_k_moba': gain((L, MOBA_HEAD_DIM)),
        'w_moba_o': nrm((L, MOBA_WIDTH, D_MODEL), MOBA_WIDTH ** -0.5),
        'g_mem_norm': gain((L, D_MODEL)),
        'w_mem_kv': nrm((L, D_MODEL, 2 * MEM_WIDTH), D_MODEL ** -0.5),
        'g_q_mem': gain((L, MEM_HEAD_DIM)),
        'g_k_mem': gain((L, MEM_HEAD_DIM)),
        'w_mem_o': nrm((L, MEM_WIDTH, D_MODEL), MEM_WIDTH ** -0.5),
        'w_out': nrm((L, D_MODEL, D_MODEL), D_MODEL ** -0.5),
        'g_norm2': gain((L, D_MODEL)),
        'w_router': nrm((L, D_MODEL, N_EXPERTS), D_MODEL ** -0.5),
        'b_router': nrm((L, N_EXPERTS), 0.01),
        'w_gu': nrm((L, N_EXPERTS, D_MODEL, 2 * D_FF), D_MODEL ** -0.5),
        'b_gu': nrm((L, N_EXPERTS, 2 * D_FF), 0.02),
        'w_down': nrm((L, N_EXPERTS, D_FF, D_MODEL), D_FF ** -0.5),
        'b_down': nrm((L, N_EXPERTS, D_MODEL), 0.02),
    }


def reference(x_prompt, x_sample, mem_prompt, cache_moba_k, cache_moba_v, cache_mem_k, cache_mem_v,
              state_conv, page_table, g_norm1, w_in, w_dw, b_dw, g_conv_ln, b_conv_ln, w_conv_out,
              b_conv_out, g_q_moba, g_k_moba, w_moba_o, g_mem_norm, w_mem_kv, g_q_mem, g_k_mem, w_mem_o,
              w_out, g_norm2, w_router, b_router, w_gu, b_gu, w_down, b_down):
    y_p, y_s = x_prompt, x_sample
    kp, vp, mkp, mvp, cp, ks, vs, cs = [], [], [], [], [], [], [], []
    for l in range(DEPTH):
        zc, q, k, v, qm, gates = project(y_p, g_norm1[l], w_in[l], g_q_moba[l], g_k_moba[l], g_q_mem[l])
        hist0 = jnp.zeros((y_p.shape[0], CONV_KERNEL - 1, CONV_WIDTH), y_p.dtype)
        u_conv, conv_tail = conv_branch(zc, hist0, w_dw[l], b_dw[l], g_conv_ln[l], b_conv_ln[l],
                                        w_conv_out[l], b_conv_out[l])
        o_moba = moba_prompt(q, k, v)
        mk, mv = mem_kv(mem_prompt, g_mem_norm[l], w_mem_kv[l], g_k_mem[l])
        o_mem = mem_attend(qm, mk, mv)
        y_p = merge_and_ffn(y_p, u_conv, o_moba, o_mem, gates, w_moba_o[l], w_mem_o[l], w_out[l],
                            g_norm2[l], w_router[l], b_router[l], w_gu[l], b_gu[l], w_down[l], b_down[l])
        kp.append(k)
        vp.append(v)
        mkp.append(mk)
        mvp.append(mv)
        cp.append(conv_tail)
        zc, q, k, v, qm, gates = project(y_s, g_norm1[l], w_in[l], g_q_moba[l], g_k_moba[l], g_q_mem[l])
        u_conv, conv_tail = conv_branch(zc, state_conv[l], w_dw[l], b_dw[l], g_conv_ln[l], b_conv_ln[l],
                                        w_conv_out[l], b_conv_out[l])
        o_moba = moba_sample(q, k, v, cache_moba_k, cache_moba_v, page_table, l)
        o_mem = mem_attend(qm, cache_mem_k[l], cache_mem_v[l])
        y_s = merge_and_ffn(y_s, u_conv, o_moba, o_mem, gates, w_moba_o[l], w_mem_o[l], w_out[l],
                            g_norm2[l], w_router[l], b_router[l], w_gu[l], b_gu[l], w_down[l], b_down[l])
        ks.append(k)
        vs.append(v)
        cs.append(conv_tail)
    return (y_p, y_s, jnp.stack(kp), jnp.stack(vp), jnp.stack(mkp), jnp.stack(mvp), jnp.stack(cp),
            jnp.stack(ks), jnp.stack(vs), jnp.stack(cs))
```

```python
import functools

import jax
import jax.numpy as jnp
from jax import lax
from jax.experimental import pallas as pl
from jax.experimental.pallas import tpu as pltpu

F32 = jnp.float32
BF16 = jnp.bfloat16

EPS = 1e-6
MOBA_HEADS = 8
MOBA_HEAD_DIM = 64
MOBA_WIDTH = MOBA_HEADS * MOBA_HEAD_DIM
MOBA_BLOCK = 256
MOBA_TOPK = 3
MEM_HEADS = 4
MEM_HEAD_DIM = 128
MEM_WIDTH = MEM_HEADS * MEM_HEAD_DIM
CONV_KERNEL = 31
CONV_HIST = CONV_KERNEL - 1
N_EXPERTS = 32
TOP_K = 4
SWIGLU_LIMIT = 7.0
SWIGLU_ALPHA = 1.702
MOE_BLOCK = 256

V7X_LANES = 128
V7X_SUBLANES = 8
HIST_PAD = 32
VMEM_LIMIT = 56 * 1024 * 1024


def _cparams(sem):
    return pltpu.CompilerParams(dimension_semantics=sem, vmem_limit_bytes=VMEM_LIMIT)


def _full(shape):
    n = len(shape)
    return pl.BlockSpec(shape, lambda *_: (0,) * n)


def _segment_ones(width, seg):
    r = jnp.arange(width) // seg
    return (r[:, None] == r[None, :]).astype(BF16)


def _rms(x, g):
    return x * lax.rsqrt(jnp.mean(x * x, axis=-1, keepdims=True) + EPS) * g


def _seg_rms(z, seg_ones, seg, g):
    sq = (z * z).astype(BF16)
    parts = []
    for c in range(0, z.shape[-1], 2 * V7X_LANES):
        parts.append(jnp.dot(sq[:, c:c + 2 * V7X_LANES], seg_ones[c:c + 2 * V7X_LANES, c:c + 2 * V7X_LANES],
                             preferred_element_type=F32))
    ms = jnp.concatenate(parts, axis=-1) * (1.0 / seg)
    return z * lax.rsqrt(ms + EPS) * g


def _proj_kernel(has_hist, carry, x_ref, *refs):
    if has_hist:
        hist_ref, refs = refs[0], refs[1:]
    (g1_ref, w1_ref, seg64_ref, seg128_ref, gq_ref, gk_ref, gqm_ref, wdw_ref, bdw_ref, gln_ref, bln_ref,
     k_out, v_out, q_out, kb_out, vb_out, qm_out, c_out, tail_out, xp_scr) = refs
    bt, tl, d = x_ref.shape
    rows = bt * tl
    cw = c_out.shape[-1]
    t = pl.program_id(1)

    x = x_ref[...].reshape(rows, d)
    h = _rms(x, g1_ref[...]).astype(BF16)
    z = jnp.dot(h, w1_ref[...], preferred_element_type=F32)
    o_q = 2 * cw
    o_k = o_q + MOBA_WIDTH
    o_v = o_k + MOBA_WIDTH
    o_qm = o_v + MOBA_WIDTH
    u = z[:, :cw] * jax.nn.sigmoid(z[:, cw:o_q])
    seg64 = seg64_ref[...]
    qn = _seg_rms(z[:, o_q:o_k], seg64, MOBA_HEAD_DIM, gq_ref[...])
    kn = _seg_rms(z[:, o_k:o_v], seg64, MOBA_HEAD_DIM, gk_ref[...])
    vv = z[:, o_v:o_qm]
    qmn = _seg_rms(z[:, o_qm:o_qm + MEM_WIDTH], seg128_ref[...], MEM_HEAD_DIM, gqm_ref[...])

    q_out[...] = qn.astype(BF16).reshape(bt, tl, MOBA_WIDTH)
    kb_out[...] = kn.astype(BF16).reshape(bt, tl, MOBA_WIDTH)
    vb_out[...] = vv.astype(BF16).reshape(bt, tl, MOBA_WIDTH)
    qm_out[...] = qmn.astype(BF16).reshape(bt, tl, MEM_WIDTH)
    kn3 = kn.reshape(bt, tl, MOBA_WIDTH)
    vv3 = vv.reshape(bt, tl, MOBA_WIDTH)
    for hh in range(MOBA_HEADS):
        sl = slice(hh * MOBA_HEAD_DIM, (hh + 1) * MOBA_HEAD_DIM)
        k_out[:, hh, :, :] = kn3[:, :, sl]
        v_out[:, hh, :, :] = vv3[:, :, sl]

    lo = HIST_PAD - CONV_HIST

    @pl.when(t == 0)
    def _():
        xp_scr[:, 0:HIST_PAD, :] = jnp.zeros((bt, HIST_PAD, cw), F32)
        if has_hist:
            xp_scr[:, lo:HIST_PAD, :] = hist_ref[...]

    xp_scr[:, HIST_PAD:HIST_PAD + tl, :] = u.reshape(bt, tl, cw)
    rc = min(tl, 64)
    bdw = bdw_ref[...]
    gln = gln_ref[...]
    bln = bln_ref[...]
    for r0 in range(0, tl, rc):
        acc = jnp.zeros((bt, rc, cw), F32) + bdw
        for j in range(CONV_KERNEL):
            acc = acc + wdw_ref[j:j + 1, :] * xp_scr[:, lo + j + r0:lo + j + r0 + rc, :]
        mu = jnp.mean(acc, axis=-1, keepdims=True)
        dev = acc - mu
        var = jnp.mean(dev * dev, axis=-1, keepdims=True)
        y = dev * lax.rsqrt(var + EPS) * gln + bln
        c_out[:, r0:r0 + rc, :] = (y * jax.nn.sigmoid(y)).astype(BF16)
    tail_out[...] = xp_scr[:, tl + lo:tl + HIST_PAD, :]
    if carry:
        xp_scr[:, 0:HIST_PAD, :] = xp_scr[:, tl:tl + HIST_PAD, :]


def _proj(x, hist, bt, tl, g1, w1, seg64, seg128, gq, gk, gqm, wdw, bdw, gln, bln):
    b, l, d = x.shape
    cw = wdw.shape[-1]
    n_t = l // tl
    has_hist = hist is not None
    carry = n_t > 1
    assert b % bt == 0 and l % tl == 0 and (not carry or (tl >= HIST_PAD and bt == 1))
    tok = lambda w: pl.BlockSpec((bt, tl, w), lambda i, j: (i, j, 0))
    head = pl.BlockSpec((bt, MOBA_HEADS, tl, MOBA_HEAD_DIM), lambda i, j: (i, 0, j, 0))
    in_specs = [tok(d)]
    args = [x]
    if has_hist:
        in_specs.append(pl.BlockSpec((bt, CONV_HIST, cw), lambda i, j: (i, 0, 0)))
        args.append(hist)
    consts = [g1, w1, seg64, seg128, gq, gk, gqm, wdw, bdw, gln, bln]
    in_specs += [_full(c.shape) for c in consts]
    out_shape = [
        jax.ShapeDtypeStruct((b, MOBA_HEADS, l, MOBA_HEAD_DIM), F32),
        jax.ShapeDtypeStruct((b, MOBA_HEADS, l, MOBA_HEAD_DIM), F32),
        jax.ShapeDtypeStruct((b, l, MOBA_WIDTH), BF16),
        jax.ShapeDtypeStruct((b, l, MOBA_WIDTH), BF16),
        jax.ShapeDtypeStruct((b, l, MOBA_WIDTH), BF16),
        jax.ShapeDtypeStruct((b, l, MEM_WIDTH), BF16),
        jax.ShapeDtypeStruct((b, l, cw), BF16),
        jax.ShapeDtypeStruct((b, CONV_HIST, cw), F32),
    ]
    out_specs = [head, head, tok(MOBA_WIDTH), tok(MOBA_WIDTH), tok(MOBA_WIDTH), tok(MEM_WIDTH), tok(cw),
                 pl.BlockSpec((bt, CONV_HIST, cw), lambda i, j: (i, 0, 0))]
    return pl.pallas_call(
        functools.partial(_proj_kernel, has_hist, carry),
        grid=(b // bt, n_t),
        in_specs=in_specs,
        out_specs=out_specs,
        out_shape=out_shape,
        scratch_shapes=[pltpu.VMEM((bt, HIST_PAD + tl, cw), F32)],
        compiler_params=_cparams(("parallel", "arbitrary")),
        name="proj",
    )(*args, *consts)


def _memkv_kernel(x_ref, g_ref, w_ref, seg128_ref, gk_ref, k_out, v_out):
    h = _rms(x_ref[...], g_ref[...]).astype(BF16)
    kv = jnp.dot(h, w_ref[...], preferred_element_type=F32)
    k_out[...] = _seg_rms(kv[:, :MEM_WIDTH], seg128_ref[...], MEM_HEAD_DIM, gk_ref[...])
    v_out[...] = kv[:, MEM_WIDTH:]


def _mem_kv(mem2d, tm, g, w, seg128, gk):
    n, d = mem2d.shape
    assert n % tm == 0
    spec = pl.BlockSpec((tm, MEM_WIDTH), lambda i: (i, 0))
    return pl.pallas_call(
        _memkv_kernel,
        grid=(n // tm,),
        in_specs=[pl.BlockSpec((tm, d), lambda i: (i, 0))] + [_full(c.shape) for c in (g, w, seg128, gk)],
        out_specs=[spec, spec],
        out_shape=[jax.ShapeDtypeStruct((n, MEM_WIDTH), F32)] * 2,
        compiler_params=_cparams(("parallel",)),
        name="mem_kv",
    )(mem2d, g, w, seg128, gk)


def _mematt_kernel(q_ref, k_ref, v_ref, o_ref):
    scale = MEM_HEAD_DIM ** -0.5
    for hh in range(MEM_HEADS):
        sl = slice(hh * MEM_HEAD_DIM, (hh + 1) * MEM_HEAD_DIM)
        q = q_ref[:, :, sl]
        k = k_ref[:, :, sl].astype(BF16)
        v = v_ref[:, :, sl].astype(BF16)
        s = jnp.einsum("bqd,bkd->bqk", q, k, preferred_element_type=F32) * scale
        m = jnp.max(s, axis=-1, keepdims=True)
        p = jnp.exp(s - m)
        l = jnp.sum(p, axis=-1, keepdims=True)
        o = jnp.einsum("bqk,bkd->bqd", p.astype(BF16), v, preferred_element_type=F32)
        o_ref[:, :, sl] = (o / l).astype(BF16)


def _mem_attend(qm, mk, mv, bt, tq):
    b, l, w = qm.shape
    m = mk.shape[1]
    assert b % bt == 0 and l % tq == 0
    return pl.pallas_call(
        _mematt_kernel,
        grid=(b // bt, l // tq),
        in_specs=[pl.BlockSpec((bt, tq, w), lambda i, j: (i, j, 0)),
                  pl.BlockSpec((bt, m, w), lambda i, j: (i, 0, 0)),
                  pl.BlockSpec((bt, m, w), lambda i, j: (i, 0, 0))],
        out_specs=pl.BlockSpec((bt, tq, w), lambda i, j: (i, j, 0)),
        out_shape=jax.ShapeDtypeStruct((b, l, w), BF16),
        compiler_params=_cparams(("parallel", "arbitrary")),
        name="mem_attend",
    )(qm, mk, mv)


def _moba_prompt_kernel(q_ref, k_ref, v_ref, o_ref, vt_scr, km_scr, sel_scr, acc_scr, m_scr, l_scr):
    blk = MOBA_BLOCK
    l = k_ref.shape[1]
    nb = l // blk
    qb = pl.program_id(1)
    hd = MOBA_HEAD_DIM

    @pl.when(qb == 0)
    def _():
        lane_head = lax.broadcasted_iota(jnp.int32, (MOBA_HEADS, MOBA_WIDTH), 1) // hd
        row_head = lax.broadcasted_iota(jnp.int32, (MOBA_HEADS, MOBA_WIDTH), 0)
        head_mask = (lane_head == row_head).astype(F32)
        for n in range(nb):
            vt_scr[n] = v_ref[0, n * blk:(n + 1) * blk, :].T
            kmean = jnp.mean(k_ref[0, n * blk:(n + 1) * blk, :].astype(F32), axis=0, keepdims=True)
            km_scr[n * MOBA_HEADS:(n + 1) * MOBA_HEADS, :] = (kmean * head_mask).astype(BF16)

    qt = q_ref[0].T
    gate = jnp.dot(km_scr[...], qt, preferred_element_type=F32).reshape(nb, MOBA_HEADS, blk)
    for n in range(nb):
        rank = jnp.zeros((MOBA_HEADS, blk), jnp.int32)
        for n2 in range(nb):
            if n2 == n:
                continue
            beats = (gate[n2] >= gate[n]) if n2 < n else (gate[n2] > gate[n])
            rank = rank + jnp.where(jnp.logical_and(beats, n2 < qb), 1, 0)
        sel_scr[n] = jnp.where(jnp.logical_and(rank < MOBA_TOPK, n < qb), 1.0, 0.0)

    row_in_pair = lax.broadcasted_iota(jnp.int32, (2 * hd, blk), 0) // hd
    key_idx = lax.broadcasted_iota(jnp.int32, (blk, blk), 0)
    tok_idx = lax.broadcasted_iota(jnp.int32, (blk, blk), 1)
    causal = key_idx <= tok_idx
    scale = hd ** -0.5

    def qt_head(hh):
        pair = hh // 2
        qp = qt[pair * 2 * hd:(pair + 1) * 2 * hd, :]
        return jnp.where(row_in_pair == hh % 2, qp, jnp.zeros_like(qp))

    def scores(kblk, hh):
        pair = hh // 2
        return jnp.dot(kblk[:, pair * 2 * hd:(pair + 1) * 2 * hd], qt_head(hh), preferred_element_type=F32) * scale

    k_own = k_ref[0, pl.ds(pl.multiple_of(qb * blk, blk), blk), :]
    for hh in range(MOBA_HEADS):
        s = jnp.where(causal, scores(k_own, hh), -jnp.inf)
        m = jnp.max(s, axis=0, keepdims=True)
        p = jnp.exp(s - m)
        m_scr[hh:hh + 1, :] = m
        l_scr[hh:hh + 1, :] = jnp.sum(p, axis=0, keepdims=True)
        acc_scr[hh * hd:(hh + 1) * hd, :] = jnp.dot(vt_scr[qb, hh * hd:(hh + 1) * hd, :], p.astype(BF16),
                                                   preferred_element_type=F32)

    def past_block(n, carry):
        kblk = k_ref[0, pl.ds(pl.multiple_of(n * blk, blk), blk), :]
        sel = sel_scr[n]
        for hh in range(MOBA_HEADS):
            s = jnp.where(sel[hh:hh + 1, :] > 0.0, scores(kblk, hh), -jnp.inf)
            m_old = m_scr[hh:hh + 1, :]
            m_new = jnp.maximum(m_old, jnp.max(s, axis=0, keepdims=True))
            alpha = jnp.exp(m_old - m_new)
            p = jnp.exp(s - m_new)
            m_scr[hh:hh + 1, :] = m_new
            l_scr[hh:hh + 1, :] = alpha * l_scr[hh:hh + 1, :] + jnp.sum(p, axis=0, keepdims=True)
            pv = jnp.dot(vt_scr[n, hh * hd:(hh + 1) * hd, :], p.astype(BF16), preferred_element_type=F32)
            acc_scr[hh * hd:(hh + 1) * hd, :] = alpha * acc_scr[hh * hd:(hh + 1) * hd, :] + pv
        return carry

    lax.fori_loop(0, qb, past_block, 0)

    for hh in range(MOBA_HEADS):
        acc_scr[hh * hd:(hh + 1) * hd, :] = acc_scr[hh * hd:(hh + 1) * hd, :] / l_scr[hh:hh + 1, :]
    o_ref[0] = acc_scr[...].T.astype(BF16)


def _moba_prompt(q, kb, vb):
    b, l, w = q.shape
    blk = MOBA_BLOCK
    assert l % blk == 0
    nb = l // blk
    full = pl.BlockSpec((1, l, w), lambda i, j: (i, 0, 0))
    tile = pl.BlockSpec((1, blk, w), lambda i, j: (i, j, 0))
    return pl.pallas_call(
        _moba_prompt_kernel,
        grid=(b, nb),
        in_specs=[tile, full, full],
        out_specs=tile,
        out_shape=jax.ShapeDtypeStruct((b, l, w), BF16),
        scratch_shapes=[pltpu.VMEM((nb, w, blk), BF16), pltpu.VMEM((nb * MOBA_HEADS, w), BF16),
                        pltpu.VMEM((nb, MOBA_HEADS, blk), F32), pltpu.VMEM((w, blk), F32),
                        pltpu.VMEM((MOBA_HEADS, blk), F32), pltpu.VMEM((MOBA_HEADS, blk), F32)],
        compiler_params=_cparams(("parallel", "arbitrary")),
        name="moba_prompt",
    )(q, kb, vb)


def _moba_sample_kernel(pt_ref, q_ref, kn_ref, vn_ref, ck_hbm, cv_hbm, o_ref,
                        kbuf, vbuf, ksem, vsem, m_scr, l_scr, g_scr, o_scr):
    b = pl.program_id(0)
    n_pages = pt_ref.shape[1]
    page = kbuf.shape[3]
    ppb = MOBA_BLOCK // page
    n_full = n_pages // ppb
    hd = MOBA_HEAD_DIM
    t = q_ref.shape[1]
    scale = hd ** -0.5

    def copies(n, slot):
        out = []
        for pg in range(ppb):
            pid = pt_ref[b, n * ppb + pg]
            out.append(pltpu.make_async_copy(ck_hbm.at[pid], kbuf.at[slot, pg], ksem.at[slot]))
            out.append(pltpu.make_async_copy(cv_hbm.at[pid], vbuf.at[slot, pg], vsem.at[slot]))
        return out

    for c in copies(0, 0):
        c.start()

    def heads(x):
        return jnp.stack([x[:, hh * hd:(hh + 1) * hd] for hh in range(MOBA_HEADS)], axis=0)

    q3 = heads(q_ref[0])
    q3f = q3.astype(F32)

    def block(n, carry):
        slot = n % 2

        @pl.when(n + 1 < n_full)
        def _():
            for c in copies(n + 1, 1 - slot):
                c.start()

        for c in copies(n, slot):
            c.wait()
        ksum = jnp.zeros((MOBA_HEADS, hd), F32)
        s_parts = []
        for pg in range(ppb):
            kp = kbuf[slot, pg]
            ksum = ksum + jnp.sum(kp, axis=1)
            s_parts.append(jnp.einsum("htd,hkd->htk", q3, kp.astype(BF16), preferred_element_type=F32) * scale)
        m = jnp.max(jnp.maximum(s_parts[0], s_parts[1]) if ppb == 2 else jnp.concatenate(s_parts, -1),
                    axis=-1, keepdims=True)
        lsum = jnp.zeros((MOBA_HEADS, t, 1), F32)
        o = jnp.zeros((MOBA_HEADS, t, hd), F32)
        for pg in range(ppb):
            p = jnp.exp(s_parts[pg] - m)
            lsum = lsum + jnp.sum(p, axis=-1, keepdims=True)
            o = o + jnp.einsum("htk,hkd->htd", p.astype(BF16), vbuf[slot, pg].astype(BF16),
                               preferred_element_type=F32)
        kmean = ksum * (1.0 / MOBA_BLOCK)
        m_scr[n] = m
        l_scr[n] = lsum
        g_scr[n] = jnp.sum(q3f * kmean[:, None, :], axis=-1, keepdims=True)
        o_scr[n] = o
        return carry

    lax.fori_loop(0, n_full, block, 0)

    gate = g_scr[...]
    blk_id = lax.broadcasted_iota(jnp.int32, gate.shape, 0)
    sel = jnp.zeros(gate.shape, jnp.bool_)
    for _ in range(min(MOBA_TOPK, n_full)):
        mx = jnp.max(gate, axis=0, keepdims=True)
        first = jnp.min(jnp.where(gate == mx, blk_id, n_full), axis=0, keepdims=True)
        hit = blk_id == first
        sel = jnp.logical_or(sel, hit)
        gate = jnp.where(hit, -jnp.inf, gate)

    kn3 = heads(kn_ref[0])
    vn3 = heads(vn_ref[0])
    s_new = jnp.einsum("htd,hkd->htk", q3, kn3, preferred_element_type=F32) * scale
    row = lax.broadcasted_iota(jnp.int32, s_new.shape, 1)
    col = lax.broadcasted_iota(jnp.int32, s_new.shape, 2)
    s_new = jnp.where(col <= row, s_new, -jnp.inf)
    m_new = jnp.max(s_new, axis=-1, keepdims=True)
    m_blk = jnp.where(sel, m_scr[...], -jnp.inf)
    m_fin = jnp.maximum(jnp.max(m_blk, axis=0), m_new)
    w_blk = jnp.where(sel, jnp.exp(m_blk - m_fin[None]), 0.0)
    p_new = jnp.exp(s_new - m_fin)
    l_fin = jnp.sum(w_blk * l_scr[...], axis=0) + jnp.sum(p_new, axis=-1, keepdims=True)
    o_fin = jnp.sum(w_blk * o_scr[...], axis=0) + jnp.einsum("htk,hkd->htd", p_new.astype(BF16), vn3,
                                                            preferred_element_type=F32)
    o_fin = o_fin / l_fin
    for hh in range(MOBA_HEADS):
        o_ref[0, :, hh * hd:(hh + 1) * hd] = o_fin[hh].astype(BF16)


def _moba_sample(q, kn, vn, cache_k, cache_v, page_table):
    b, t, w = q.shape
    n_pages = page_table.shape[1]
    _, heads, page, hd = cache_k.shape
    ppb = MOBA_BLOCK // page
    assert MOBA_BLOCK % page == 0 and n_pages % ppb == 0 and heads == MOBA_HEADS and hd == MOBA_HEAD_DIM
    n_full = n_pages // ppb
    tok = pl.BlockSpec((1, t, w), lambda i, pt: (i, 0, 0))
    any_spec = pl.BlockSpec(memory_space=pl.ANY)
    grid_spec = pltpu.PrefetchScalarGridSpec(
        num_scalar_prefetch=1,
        grid=(b,),
        in_specs=[tok, tok, tok, any_spec, any_spec],
        out_specs=tok,
        scratch_shapes=[pltpu.VMEM((2, ppb, heads, page, hd), F32), pltpu.VMEM((2, ppb, heads, page, hd), F32),
                        pltpu.SemaphoreType.DMA((2,)), pltpu.SemaphoreType.DMA((2,)),
                        pltpu.VMEM((n_full, heads, t, 1), F32), pltpu.VMEM((n_full, heads, t, 1), F32),
                        pltpu.VMEM((n_full, heads, t, 1), F32), pltpu.VMEM((n_full, heads, t, hd), F32)],
    )
    return pl.pallas_call(
        _moba_sample_kernel,
        grid_spec=grid_spec,
        out_shape=jax.ShapeDtypeStruct((b, t, w), BF16),
        compiler_params=_cparams(("arbitrary",)),
        name="moba_sample",
    )(page_table, q, kn, vn, cache_k, cache_v)


def _merge_kernel(x_ref, c_ref, om_ref, omem_ref, g1_ref, wg_ref, wco_ref, bco_ref, wmo_ref, wmemo_ref, wout_ref,
                  g2_ref, wr_hi_ref, wr_lo_ref, br_ref, x1_out, h2_out, e_out, gate_out):
    tm, d = x_ref.shape
    x = x_ref[...]
    h = _rms(x, g1_ref[...]).astype(BF16)
    gates = jax.nn.sigmoid(jnp.dot(h, wg_ref[...], preferred_element_type=F32))
    u_conv = jnp.dot(c_ref[...], wco_ref[...], preferred_element_type=F32) + bco_ref[...]
    u_moba = jnp.dot(om_ref[...], wmo_ref[...], preferred_element_type=F32)
    u_mem = jnp.dot(omem_ref[...], wmemo_ref[...], preferred_element_type=F32)
    merged = gates[:, :d] * u_conv + gates[:, d:2 * d] * u_moba + gates[:, 2 * d:] * u_mem
    x1 = x + jnp.dot(merged.astype(BF16), wout_ref[...], preferred_element_type=F32)
    x1_out[...] = x1
    h2 = _rms(x1, g2_ref[...])
    for c in range(d // V7X_LANES):
        h2_out[:, c, :] = h2[:, c * V7X_LANES:(c + 1) * V7X_LANES]

    h2_hi = h2.astype(BF16)
    h2_lo = (h2 - h2_hi.astype(F32)).astype(BF16)
    nt = (((1,), (1,)), ((), ()))
    logits = (lax.dot_general(wr_hi_ref[...], h2_hi, nt, preferred_element_type=F32)
              + lax.dot_general(wr_hi_ref[...], h2_lo, nt, preferred_element_type=F32)
              + lax.dot_general(wr_lo_ref[...], h2_hi, nt, preferred_element_type=F32)) + br_ref[...]
    e_id = lax.broadcasted_iota(jnp.int32, logits.shape, 0)
    tops, ids = [], []
    for _ in range(TOP_K):
        mx = jnp.max(logits, axis=0, keepdims=True)
        first = jnp.min(jnp.where(logits == mx, e_id, N_EXPERTS), axis=0, keepdims=True)
        tops.append(mx)
        ids.append(first)
        logits = jnp.where(e_id == first, -jnp.inf, logits)
    ex = [jnp.exp(v - tops[0]) for v in tops]
    denom = ex[0] + ex[1] + ex[2] + ex[3]
    e_out[...] = jnp.concatenate(ids, axis=0)
    gate_rows = jnp.concatenate([v / denom for v in ex] + [jnp.zeros((V7X_LANES - TOP_K, tm), F32)], axis=0)
    gate_out[...] = gate_rows.T


def _merge(x2d, c, om, omem, tm, consts):
    n, d = x2d.shape
    assert n % tm == 0
    row = lambda w: pl.BlockSpec((tm, w), lambda i: (i, 0))
    n_chunk = d // V7X_LANES
    return pl.pallas_call(
        _merge_kernel,
        grid=(n // tm,),
        in_specs=[row(d), row(c.shape[1]), row(om.shape[1]), row(omem.shape[1])] + [_full(a.shape) for a in consts],
        out_specs=[row(d), pl.BlockSpec((tm, n_chunk, V7X_LANES), lambda i: (i, 0, 0)),
                   pl.BlockSpec((TOP_K, tm), lambda i: (0, i)), row(V7X_LANES)],
        out_shape=[jax.ShapeDtypeStruct((n, d), F32), jax.ShapeDtypeStruct((n, n_chunk, V7X_LANES), F32),
                   jax.ShapeDtypeStruct((TOP_K, n), jnp.int32), jax.ShapeDtypeStruct((n, V7X_LANES), F32)],
        compiler_params=_cparams(("parallel",)),
        name="merge",
    )(x2d, c, om, omem, *consts)


META_ROWS = V7X_SUBLANES
TOK_ROW0 = 0
DST_ROW0 = MOE_BLOCK // V7X_LANES


def _moe_kernel(be_ref, meta_hbm, h_hbm, wgu_ref, bgu_ref, wd_ref, bd_ref, o_hbm,
                meta_smem, msem, xbuf, xsem, ybuf, ysem):
    i = pl.program_id(0)
    n = pl.num_programs(0)
    rows = MOE_BLOCK
    n_chunk = xbuf.shape[2]

    def meta_copy(blk):
        return pltpu.make_async_copy(meta_hbm.at[pl.ds(pl.multiple_of(blk * META_ROWS, META_ROWS), META_ROWS)],
                                     meta_smem.at[blk % 3], msem.at[blk % 3])

    def gather_start(blk):
        ms, slot = blk % 3, blk % 2

        def body(r, carry):
            tok = meta_smem[ms, TOK_ROW0 + r // V7X_LANES, r % V7X_LANES]
            pltpu.make_async_copy(h_hbm.at[tok], xbuf.at[slot, r], xsem.at[slot]).start()
            return carry

        lax.fori_loop(0, rows, body, 0, unroll=8)

    def scatter_start(blk):
        ms, slot = blk % 3, blk % 2

        def body(r, carry):
            dst = meta_smem[ms, DST_ROW0 + r // V7X_LANES, r % V7X_LANES]
            pltpu.make_async_copy(ybuf.at[slot, r], o_hbm.at[dst], ysem.at[slot]).start()
            return carry

        lax.fori_loop(0, rows, body, 0, unroll=8)

    def gather_wait(slot):
        pltpu.make_async_copy(h_hbm.at[pl.ds(0, rows)], xbuf.at[slot], xsem.at[slot]).wait()

    def scatter_wait(slot):
        pltpu.make_async_copy(ybuf.at[slot], o_hbm.at[pl.ds(0, rows)], ysem.at[slot]).wait()

    @pl.when(i == 0)
    def _():
        meta_copy(0).start()
        meta_copy(0).wait()
        gather_start(0)

        @pl.when(n > 1)
        def _():
            meta_copy(1).start()

    @pl.when(i + 2 < n)
    def _():
        meta_copy(i + 2).start()

    @pl.when(i + 1 < n)
    def _():
        meta_copy(i + 1).wait()
        gather_start(i + 1)

    slot = i % 2
    gather_wait(slot)

    @pl.when(i >= 2)
    def _():
        scatter_wait(slot)

    x = jnp.concatenate([xbuf[slot, :, c, :] for c in range(n_chunk)], axis=-1).astype(BF16)
    gu = jnp.dot(x, wgu_ref[0], preferred_element_type=F32) + bgu_ref[0]
    dff = gu.shape[-1] // 2
    g = jnp.minimum(gu[:, :dff], SWIGLU_LIMIT)
    u = jnp.clip(gu[:, dff:], -SWIGLU_LIMIT, SWIGLU_LIMIT)
    act = (g * jax.nn.sigmoid(SWIGLU_ALPHA * g) * (u + 1.0)).astype(BF16)
    y = jnp.dot(act, wd_ref[0], preferred_element_type=F32) + bd_ref[0]
    for c in range(n_chunk):
        ybuf[slot, :, c, :] = y[:, c * V7X_LANES:(c + 1) * V7X_LANES]
    scatter_start(i)

    @pl.when(i == n - 1)
    def _():
        scatter_wait(slot)

        @pl.when(n > 1)
        def _():
            scatter_wait(1 - slot)


def _moe(block_e, meta, h_rows, n_out_rows, wgu, bgu, wd, bd):
    n_blocks = block_e.shape[0]
    _, n_chunk, lanes = h_rows.shape
    d = n_chunk * lanes
    any_spec = pl.BlockSpec(memory_space=pl.ANY)
    grid_spec = pltpu.PrefetchScalarGridSpec(
        num_scalar_prefetch=1,
        grid=(n_blocks,),
        in_specs=[any_spec, any_spec,
                  pl.BlockSpec((1, d, wgu.shape[2]), lambda i, be: (be[i], 0, 0)),
                  pl.BlockSpec((1, 1, bgu.shape[2]), lambda i, be: (be[i], 0, 0)),
                  pl.BlockSpec((1, wd.shape[1], d), lambda i, be: (be[i], 0, 0)),
                  pl.BlockSpec((1, 1, d), lambda i, be: (be[i], 0, 0))],
        out_specs=any_spec,
        scratch_shapes=[pltpu.SMEM((3, META_ROWS, V7X_LANES), jnp.int32), pltpu.SemaphoreType.DMA((3,)),
                        pltpu.VMEM((2, MOE_BLOCK, n_chunk, lanes), F32), pltpu.SemaphoreType.DMA((2,)),
                        pltpu.VMEM((2, MOE_BLOCK, n_chunk, lanes), F32), pltpu.SemaphoreType.DMA((2,))],
    )
    return pl.pallas_call(
        _moe_kernel,
        grid_spec=grid_spec,
        out_shape=jax.ShapeDtypeStruct((n_out_rows, n_chunk, lanes), F32),
        compiler_params=_cparams(("arbitrary",)),
        name="moe",
    )(block_e, meta, h_rows, wgu, bgu, wd, bd)


def _route(top_e, n_tok):
    nk = n_tok * TOP_K
    flat_e = top_e.reshape(nk)
    order = jnp.argsort(flat_e).astype(jnp.int32)
    e_sorted = flat_e[order]
    counts = jnp.bincount(flat_e, length=N_EXPERTS).astype(jnp.int32)
    padded = (counts + MOE_BLOCK - 1) // MOE_BLOCK * MOE_BLOCK
    pad_end = jnp.cumsum(padded)
    pad_start = pad_end - padded
    grp_start = jnp.cumsum(counts) - counts
    dest = pad_start[e_sorted] + jnp.arange(nk, dtype=jnp.int32) - grp_start[e_sorted]
    n_blocks = (nk + N_EXPERTS * (MOE_BLOCK - 1) + MOE_BLOCK - 1) // MOE_BLOCK
    n_rows = n_blocks * MOE_BLOCK
    row_tok = jnp.zeros((n_rows,), jnp.int32).at[dest].set(order % n_tok)
    is_real = jnp.zeros((n_rows,), jnp.bool_).at[dest].set(True)
    spill = nk + jnp.cumsum(jnp.logical_not(is_real).astype(jnp.int32)) - 1
    row_dst = jnp.where(is_real, jnp.zeros((n_rows,), jnp.int32).at[dest].set(order), spill)
    block_e = jnp.minimum(jnp.searchsorted(pad_end, jnp.arange(n_blocks, dtype=jnp.int32) * MOE_BLOCK, side="right"),
                          N_EXPERTS - 1).astype(jnp.int32)
    per = MOE_BLOCK // V7X_LANES
    meta = jnp.concatenate([row_tok.reshape(n_blocks, per, V7X_LANES), row_dst.reshape(n_blocks, per, V7X_LANES),
                            jnp.zeros((n_blocks, META_ROWS - 2 * per, V7X_LANES), jnp.int32)], axis=1)
    return block_e, meta.reshape(n_blocks * META_ROWS, V7X_LANES), n_rows


def _combine_kernel(x1_ref, gate_ref, o0_ref, o1_ref, o2_ref, o3_ref, y_ref):
    gate = gate_ref[...]
    n_chunk = o0_ref.shape[1]
    cols = [gate[:, k:k + 1] for k in range(TOP_K)]
    for c in range(n_chunk):
        sl = slice(c * V7X_LANES, (c + 1) * V7X_LANES)
        acc = x1_ref[:, sl]
        for k, o_ref in enumerate((o0_ref, o1_ref, o2_ref, o3_ref)):
            acc = acc + cols[k] * o_ref[:, c, :]
        y_ref[:, sl] = acc


def _combine(x1, gate_col, o_rows, tm):
    n, d = x1.shape
    _, n_chunk, lanes = o_rows.shape
    assert n % tm == 0
    per = n // tm
    o_spec = lambda k: pl.BlockSpec((tm, n_chunk, lanes), lambda i, k=k: (k * per + i, 0, 0))
    return pl.pallas_call(
        _combine_kernel,
        grid=(per,),
        in_specs=[pl.BlockSpec((tm, d), lambda i: (i, 0)), pl.BlockSpec((tm, V7X_LANES), lambda i: (i, 0))]
                 + [o_spec(k) for k in range(TOP_K)],
        out_specs=pl.BlockSpec((tm, d), lambda i: (i, 0)),
        out_shape=jax.ShapeDtypeStruct((n, d), F32),
        compiler_params=_cparams(("parallel",)),
        name="combine",
    )(x1, gate_col, o_rows, o_rows, o_rows, o_rows)


def _prep_weights(p):
    cw = p["w_dw"].shape[-1]
    o_gate = 2 * cw + 3 * MOBA_WIDTH + MEM_WIDTH
    row = lambda a: a.reshape(1, -1).astype(F32)
    tile = lambda g, n: jnp.tile(g.astype(F32), n).reshape(1, -1)
    prep = {}
    prep["proj"] = (
        row(p["g_norm1"]), p["w_in"][:, :o_gate].astype(BF16),
        _segment_ones(MOBA_WIDTH, MOBA_HEAD_DIM), _segment_ones(MEM_WIDTH, MEM_HEAD_DIM),
        tile(p["g_q_moba"], MOBA_HEADS), tile(p["g_k_moba"], MOBA_HEADS), tile(p["g_q_mem"], MEM_HEADS),
        p["w_dw"].astype(F32), row(p["b_dw"]), row(p["g_conv_ln"]), row(p["b_conv_ln"]),
    )
    prep["mem_kv"] = (row(p["g_mem_norm"]), p["w_mem_kv"].astype(BF16), _segment_ones(MEM_WIDTH, MEM_HEAD_DIM),
                      tile(p["g_k_mem"], MEM_HEADS))
    wr_t = p["w_router"].astype(F32).T
    wr_hi = wr_t.astype(BF16)
    wr_lo = (wr_t - wr_hi.astype(F32)).astype(BF16)
    prep["merge"] = (
        row(p["g_norm1"]), p["w_in"][:, o_gate:].astype(BF16), p["w_conv_out"].astype(BF16), row(p["b_conv_out"]),
        p["w_moba_o"].astype(BF16), p["w_mem_o"].astype(BF16), p["w_out"].astype(BF16), row(p["g_norm2"]),
        wr_hi, wr_lo, p["b_router"].astype(F32).reshape(-1, 1),
    )
    prep["moe"] = (p["w_gu"].astype(BF16), p["b_gu"].astype(F32)[:, None, :], p["w_down"].astype(BF16),
                   p["b_down"].astype(F32)[:, None, :])
    return prep


def _ffn(x2d, c, om, omem, tm, prep):
    n = x2d.shape[0]
    x1, h_rows, top_e, gate_col = _merge(x2d, c, om, omem, tm, prep["merge"])
    block_e, meta, n_rows = _route(top_e, n)
    o_rows = _moe(block_e, meta, h_rows, n_rows, *prep["moe"])
    return _combine(x1, gate_col, o_rows, tm)


_PARAM_NAMES = ("g_norm1", "w_in", "w_dw", "b_dw", "g_conv_ln", "b_conv_ln", "w_conv_out", "b_conv_out", "g_q_moba",
                "g_k_moba", "w_moba_o", "g_mem_norm", "w_mem_kv", "g_q_mem", "g_k_mem", "w_mem_o", "w_out", "g_norm2",
                "w_router", "b_router", "w_gu", "b_gu", "w_down", "b_down")

PROMPT_ROWS = 512
SAMPLE_BATCH_TILE = 64
SAMPLE_MEM_TILE = 8


def kernel(x_prompt, x_sample, mem_prompt, cache_moba_k, cache_moba_v, cache_mem_k, cache_mem_v, state_conv, page_table, g_norm1, w_in, w_dw, b_dw, g_conv_ln, b_conv_ln, w_conv_out, b_conv_out, g_q_moba, g_k_moba, w_moba_o, g_mem_norm, w_mem_kv, g_q_mem, g_k_mem, w_mem_o, w_out, g_norm2, w_router, b_router, w_gu, b_gu, w_down, b_down):
    params = dict(zip(_PARAM_NAMES, (g_norm1, w_in, w_dw, b_dw, g_conv_ln, b_conv_ln, w_conv_out, b_conv_out,
                                     g_q_moba, g_k_moba, w_moba_o, g_mem_norm, w_mem_kv, g_q_mem, g_k_mem, w_mem_o,
                                     w_out, g_norm2, w_router, b_router, w_gu, b_gu, w_down, b_down)))
    depth = g_norm1.shape[0]
    bp, lp, d = x_prompt.shape
    bs, ls, _ = x_sample.shape
    n_mem = mem_prompt.shape[1]
    y_p, y_s = x_prompt, x_sample
    outs = [[] for _ in range(8)]
    for layer in range(depth):
        prep = _prep_weights({k: v[layer] for k, v in params.items()})
        k, v, q, kb, vb, qm, c, tail = _proj(y_p, None, 1, PROMPT_ROWS, *prep["proj"])
        mk, mv = _mem_kv(mem_prompt.reshape(bp * n_mem, d), PROMPT_ROWS, *prep["mem_kv"])
        o_moba = _moba_prompt(q, kb, vb)
        o_mem = _mem_attend(qm, mk.reshape(bp, n_mem, MEM_WIDTH), mv.reshape(bp, n_mem, MEM_WIDTH), 1, PROMPT_ROWS)
        flat = lambda a: a.reshape(bp * lp, a.shape[-1])
        y_p = _ffn(flat(y_p), flat(c), flat(o_moba), flat(o_mem), PROMPT_ROWS, prep).reshape(bp, lp, d)
        mem_shape = (bp, n_mem, MEM_HEADS, MEM_HEAD_DIM)
        for lst, val in zip(outs[:5], (k, v, mk.reshape(mem_shape), mv.reshape(mem_shape), tail)):
            lst.append(val)
        k, v, q, kb, vb, qm, c, tail = _proj(y_s, state_conv[layer], SAMPLE_BATCH_TILE, ls, *prep["proj"])
        o_moba = _moba_sample(q, kb, vb, cache_moba_k[layer], cache_moba_v[layer], page_table)
        o_mem = _mem_attend(qm, cache_mem_k[layer].reshape(bs, n_mem, MEM_WIDTH),
                            cache_mem_v[layer].reshape(bs, n_mem, MEM_WIDTH), SAMPLE_MEM_TILE, ls)
        flat = lambda a: a.reshape(bs * ls, a.shape[-1])
        y_s = _ffn(flat(y_s), flat(c), flat(o_moba), flat(o_mem), PROMPT_ROWS, prep).reshape(bs, ls, d)
        for lst, val in zip(outs[5:], (k, v, tail)):
            lst.append(val)
    kp, vp, mkp, mvp, cp, ks, vs, cs = (jnp.stack(o) for o in outs)
    return (y_p, y_s, kp, vp, mkp, mvp, cp, ks, vs, cs)
```

```python
import functools

import jax
import jax.numpy as jnp
from jax import lax
from jax.experimental import pallas as pl
from jax.experimental.pallas import tpu as pltpu

F32 = jnp.float32
BF16 = jnp.bfloat16

EPS = 1e-6
MOBA_HEADS = 8
MOBA_HEAD_DIM = 64
MOBA_WIDTH = MOBA_HEADS * MOBA_HEAD_DIM
MOBA_BLOCK = 256
MOBA_TOPK = 3
MEM_HEADS = 4
MEM_HEAD_DIM = 128
MEM_WIDTH = MEM_HEADS * MEM_HEAD_DIM
CONV_KERNEL = 31
CONV_HIST = CONV_KERNEL - 1
N_EXPERTS = 32
TOP_K = 4
SWIGLU_LIMIT = 7.0
SWIGLU_ALPHA = 1.702
MOE_BLOCK = 256

V7X_LANES = 128
V7X_SUBLANES = 8
HIST_PAD = 32
VMEM_LIMIT = 56 * 1024 * 1024


def _cparams(sem):
    return pltpu.CompilerParams(dimension_semantics=sem, vmem_limit_bytes=VMEM_LIMIT)


def _full(shape):
    n = len(shape)
    return pl.BlockSpec(shape, lambda *_: (0,) * n)


def _segment_ones(width, seg):
    r = jnp.arange(width) // seg
    return (r[:, None] == r[None, :]).astype(BF16)


def _rms(x, g):
    return x * lax.rsqrt(jnp.mean(x * x, axis=-1, keepdims=True) + EPS) * g


def _seg_rms(z, seg_ones, seg, g):
    sq = (z * z).astype(BF16)
    parts = []
    for c in range(0, z.shape[-1], 2 * V7X_LANES):
        parts.append(jnp.dot(sq[:, c:c + 2 * V7X_LANES], seg_ones[c:c + 2 * V7X_LANES, c:c + 2 * V7X_LANES],
                             preferred_element_type=F32))
    ms = jnp.concatenate(parts, axis=-1) * (1.0 / seg)
    return z * lax.rsqrt(ms + EPS) * g


def _proj_kernel(has_hist, carry, x_ref, *refs):
    if has_hist:
        hist_ref, refs = refs[0], refs[1:]
    (g1_ref, w1_ref, seg64_ref, seg128_ref, gq_ref, gk_ref, gqm_ref, wdw_ref, bdw_ref, gln_ref, bln_ref,
     k_out, v_out, q_out, kb_out, vb_out, qm_out, c_out, tail_out, xp_scr) = refs
    bt, tl, d = x_ref.shape
    rows = bt * tl
    cw = c_out.shape[-1]
    t = pl.program_id(1)

    x = x_ref[...].reshape(rows, d)
    h = _rms(x, g1_ref[...]).astype(BF16)
    z = jnp.dot(h, w1_ref[...], preferred_element_type=F32)
    o_q = 2 * cw
    o_k = o_q + MOBA_WIDTH
    o_v = o_k + MOBA_WIDTH
    o_qm = o_v + MOBA_WIDTH
    u = z[:, :cw] * jax.nn.sigmoid(z[:, cw:o_q])
    seg64 = seg64_ref[...]
    qn = _seg_rms(z[:, o_q:o_k], seg64, MOBA_HEAD_DIM, gq_ref[...])
    kn = _seg_rms(z[:, o_k:o_v], seg64, MOBA_HEAD_DIM, gk_ref[...])
    vv = z[:, o_v:o_qm]
    qmn = _seg_rms(z[:, o_qm:o_qm + MEM_WIDTH], seg128_ref[...], MEM_HEAD_DIM, gqm_ref[...])

    q_out[...] = qn.astype(BF16).reshape(bt, tl, MOBA_WIDTH)
    kb_out[...] = kn.astype(BF16).reshape(bt, tl, MOBA_WIDTH)
    vb_out[...] = vv.astype(BF16).reshape(bt, tl, MOBA_WIDTH)
    qm_out[...] = qmn.astype(BF16).reshape(bt, tl, MEM_WIDTH)
    kn3 = kn.reshape(bt, tl, MOBA_WIDTH)
    vv3 = vv.reshape(bt, tl, MOBA_WIDTH)
    for hh in range(MOBA_HEADS):
        sl = slice(hh * MOBA_HEAD_DIM, (hh + 1) * MOBA_HEAD_DIM)
        k_out[:, hh, :, :] = kn3[:, :, sl]
        v_out[:, hh, :, :] = vv3[:, :, sl]

    lo = HIST_PAD - CONV_HIST

    @pl.when(t == 0)
    def _():
        xp_scr[:, 0:HIST_PAD, :] = jnp.zeros((bt, HIST_PAD, cw), F32)
        if has_hist:
            xp_scr[:, lo:HIST_PAD, :] = hist_ref[...]

    xp_scr[:, HIST_PAD:HIST_PAD + tl, :] = u.reshape(bt, tl, cw)
    rc = min(tl, 64)
    bdw = bdw_ref[...]
    gln = gln_ref[...]
    bln = bln_ref[...]
    for r0 in range(0, tl, rc):
        window = xp_scr[:, r0:r0 + rc + HIST_PAD, :]
        acc = jnp.zeros((bt, rc, cw), F32) + bdw
        for s in range(V7X_SUBLANES):
            ext = rc if s == 0 else rc + V7X_SUBLANES
            part = None
            for q in range(HIST_PAD // V7X_SUBLANES + 1):
                j = V7X_SUBLANES * q + s - lo
                if 0 <= j < CONV_KERNEL:
                    term = wdw_ref[j:j + 1, :] * window[:, V7X_SUBLANES * q:V7X_SUBLANES * q + ext, :]
                    part = term if part is None else part + term
            acc = acc + (part if s == 0 else part[:, s:s + rc, :])
        mu = jnp.mean(acc, axis=-1, keepdims=True)
        dev = acc - mu
        var = jnp.mean(dev * dev, axis=-1, keepdims=True)
        y = dev * lax.rsqrt(var + EPS) * gln + bln
        c_out[:, r0:r0 + rc, :] = (y * jax.nn.sigmoid(y)).astype(BF16)
    tail_out[...] = xp_scr[:, tl + lo:tl + HIST_PAD, :]
    if carry:
        xp_scr[:, 0:HIST_PAD, :] = xp_scr[:, tl:tl + HIST_PAD, :]


def _proj(x, hist, bt, tl, g1, w1, seg64, seg128, gq, gk, gqm, wdw, bdw, gln, bln):
    b, l, d = x.shape
    cw = wdw.shape[-1]
    n_t = l // tl
    has_hist = hist is not None
    carry = n_t > 1
    assert b % bt == 0 and l % tl == 0 and (not carry or (tl >= HIST_PAD and bt == 1))
    tok = lambda w: pl.BlockSpec((bt, tl, w), lambda i, j: (i, j, 0))
    head = pl.BlockSpec((bt, MOBA_HEADS, tl, MOBA_HEAD_DIM), lambda i, j: (i, 0, j, 0))
    in_specs = [tok(d)]
    args = [x]
    if has_hist:
        in_specs.append(pl.BlockSpec((bt, CONV_HIST, cw), lambda i, j: (i, 0, 0)))
        args.append(hist)
    consts = [g1, w1, seg64, seg128, gq, gk, gqm, wdw, bdw, gln, bln]
    in_specs += [_full(c.shape) for c in consts]
    out_shape = [
        jax.ShapeDtypeStruct((b, MOBA_HEADS, l, MOBA_HEAD_DIM), F32),
        jax.ShapeDtypeStruct((b, MOBA_HEADS, l, MOBA_HEAD_DIM), F32),
        jax.ShapeDtypeStruct((b, l, MOBA_WIDTH), BF16),
        jax.ShapeDtypeStruct((b, l, MOBA_WIDTH), BF16),
        jax.ShapeDtypeStruct((b, l, MOBA_WIDTH), BF16),
        jax.ShapeDtypeStruct((b, l, MEM_WIDTH), BF16),
        jax.ShapeDtypeStruct((b, l, cw), BF16),
        jax.ShapeDtypeStruct((b, CONV_HIST, cw), F32),
    ]
    out_specs = [head, head, tok(MOBA_WIDTH), tok(MOBA_WIDTH), tok(MOBA_WIDTH), tok(MEM_WIDTH), tok(cw),
                 pl.BlockSpec((bt, CONV_HIST, cw), lambda i, j: (i, 0, 0))]
    return pl.pallas_call(
        functools.partial(_proj_kernel, has_hist, carry),
        grid=(b // bt, n_t),
        in_specs=in_specs,
        out_specs=out_specs,
        out_shape=out_shape,
        scratch_shapes=[pltpu.VMEM((bt, HIST_PAD + tl, cw), F32)],
        compiler_params=_cparams(("parallel", "arbitrary")),
        name="proj",
    )(*args, *consts)


def _memkv_kernel(x_ref, g_ref, w_ref, seg128_ref, gk_ref, k_out, v_out):
    h = _rms(x_ref[...], g_ref[...]).astype(BF16)
    kv = jnp.dot(h, w_ref[...], preferred_element_type=F32)
    k_out[...] = _seg_rms(kv[:, :MEM_WIDTH], seg128_ref[...], MEM_HEAD_DIM, gk_ref[...])
    v_out[...] = kv[:, MEM_WIDTH:]


def _mem_kv(mem2d, tm, g, w, seg128, gk):
    n, d = mem2d.shape
    assert n % tm == 0
    spec = pl.BlockSpec((tm, MEM_WIDTH), lambda i: (i, 0))
    return pl.pallas_call(
        _memkv_kernel,
        grid=(n // tm,),
        in_specs=[pl.BlockSpec((tm, d), lambda i: (i, 0))] + [_full(c.shape) for c in (g, w, seg128, gk)],
        out_specs=[spec, spec],
        out_shape=[jax.ShapeDtypeStruct((n, MEM_WIDTH), F32)] * 2,
        compiler_params=_cparams(("parallel",)),
        name="mem_kv",
    )(mem2d, g, w, seg128, gk)


def _mematt_kernel(q_ref, k_ref, v_ref, o_ref):
    scale = MEM_HEAD_DIM ** -0.5
    for hh in range(MEM_HEADS):
        sl = slice(hh * MEM_HEAD_DIM, (hh + 1) * MEM_HEAD_DIM)
        q = q_ref[:, :, sl]
        k = k_ref[0, :, :, sl].astype(BF16)
        v = v_ref[0, :, :, sl].astype(BF16)
        s = jnp.einsum("bqd,bkd->bqk", q, k, preferred_element_type=F32) * scale
        m = jnp.max(s, axis=-1, keepdims=True)
        p = jnp.exp(s - m)
        l = jnp.sum(p, axis=-1, keepdims=True)
        o = jnp.einsum("bqk,bkd->bqd", p.astype(BF16), v, preferred_element_type=F32)
        o_ref[:, :, sl] = (o / l).astype(BF16)


def _mem_attend(qm, mk, mv, layer, bt, tq):
    b, l, w = qm.shape
    m = mk.shape[2]
    assert b % bt == 0 and l % tq == 0
    return pl.pallas_call(
        _mematt_kernel,
        grid=(b // bt, l // tq),
        in_specs=[pl.BlockSpec((bt, tq, w), lambda i, j: (i, j, 0)),
                  pl.BlockSpec((1, bt, m, w), lambda i, j: (layer, i, 0, 0)),
                  pl.BlockSpec((1, bt, m, w), lambda i, j: (layer, i, 0, 0))],
        out_specs=pl.BlockSpec((bt, tq, w), lambda i, j: (i, j, 0)),
        out_shape=jax.ShapeDtypeStruct((b, l, w), BF16),
        compiler_params=_cparams(("parallel", "arbitrary")),
        name="mem_attend",
    )(qm, mk, mv)


def _moba_prompt_kernel(q_ref, k_ref, v_ref, o_ref, vt_scr, km_scr, sel_scr, acc_scr, m_scr, l_scr, wq_scr):
    blk = MOBA_BLOCK
    l = k_ref.shape[1]
    nb = l // blk
    qb = pl.program_id(1)
    hd = MOBA_HEAD_DIM

    @pl.when(qb == 0)
    def _():
        lane_head = lax.broadcasted_iota(jnp.int32, (MOBA_HEADS, MOBA_WIDTH), 1) // hd
        row_head = lax.broadcasted_iota(jnp.int32, (MOBA_HEADS, MOBA_WIDTH), 0)
        head_mask = (lane_head == row_head).astype(F32)
        for n in range(nb):
            vt_scr[n] = v_ref[0, n * blk:(n + 1) * blk, :].T
            kmean = jnp.mean(k_ref[0, n * blk:(n + 1) * blk, :].astype(F32), axis=0, keepdims=True)
            km_scr[n * MOBA_HEADS:(n + 1) * MOBA_HEADS, :] = (kmean * head_mask).astype(BF16)

    qt = q_ref[0].T
    gate = jnp.dot(km_scr[...], qt, preferred_element_type=F32).reshape(nb, MOBA_HEADS, blk)
    for n in range(nb):
        rank = jnp.zeros((MOBA_HEADS, blk), jnp.int32)
        for n2 in range(nb):
            if n2 == n:
                continue
            beats = (gate[n2] >= gate[n]) if n2 < n else (gate[n2] > gate[n])
            rank = rank + jnp.where(jnp.logical_and(beats, n2 < qb), 1, 0)
        sel_scr[n] = jnp.where(jnp.logical_and(rank < MOBA_TOPK, n < qb), 1.0, 0.0)

    pairs = MOBA_HEADS // 2
    row_in_pair = lax.broadcasted_iota(jnp.int32, (2 * hd, blk), 0) // hd
    for pr in range(pairs):
        qp = qt[pr * 2 * hd:(pr + 1) * 2 * hd, :] * jnp.asarray(hd ** -0.5, BF16)
        zero = jnp.zeros_like(qp)
        wq_scr[pr] = jnp.concatenate([jnp.where(row_in_pair == 0, qp, zero), jnp.where(row_in_pair == 1, qp, zero)],
                                     axis=1)

    def attend(kblk, n, keep, first):
        s_pairs = [jnp.dot(kblk[:, pr * 2 * hd:(pr + 1) * 2 * hd], wq_scr[pr], preferred_element_type=F32)
                   for pr in range(pairs)]
        m_old = None if first else m_scr[...]
        l_old = None if first else l_scr[...]
        m_rows, l_rows, alphas, probs = [], [], [], []
        for hh in range(MOBA_HEADS):
            s = jnp.where(keep(hh), s_pairs[hh // 2][:, (hh % 2) * blk:(hh % 2 + 1) * blk], -jnp.inf)
            m_new = jnp.max(s, axis=0, keepdims=True)
            if not first:
                m_new = jnp.maximum(m_old[hh:hh + 1, :], m_new)
                alphas.append(jnp.exp(m_old[hh:hh + 1, :] - m_new))
            p = jnp.exp(s - m_new)
            p_sum = jnp.sum(p, axis=0, keepdims=True)
            l_rows.append(p_sum if first else alphas[hh] * l_old[hh:hh + 1, :] + p_sum)
            m_rows.append(m_new)
            probs.append(p.astype(BF16))
        m_scr[...] = jnp.concatenate(m_rows, axis=0)
        l_scr[...] = jnp.concatenate(l_rows, axis=0)
        pvs = [jnp.dot(vt_scr[n, hh * hd:(hh + 1) * hd, :], probs[hh], preferred_element_type=F32)
               for hh in range(MOBA_HEADS)]
        for hh in range(MOBA_HEADS):
            rows = slice(hh * hd, (hh + 1) * hd)
            acc_scr[rows, :] = pvs[hh] if first else alphas[hh] * acc_scr[rows, :] + pvs[hh]

    causal = lax.broadcasted_iota(jnp.int32, (blk, blk), 0) <= lax.broadcasted_iota(jnp.int32, (blk, blk), 1)
    attend(k_ref[0, pl.ds(pl.multiple_of(qb * blk, blk), blk), :], qb, lambda hh: causal, True)

    def past_block(n, carry):
        sel = sel_scr[n]
        attend(k_ref[0, pl.ds(pl.multiple_of(n * blk, blk), blk), :], n, lambda hh: sel[hh:hh + 1, :] > 0.0, False)
        return carry

    lax.fori_loop(0, qb, past_block, 0)

    for hh in range(MOBA_HEADS):
        acc_scr[hh * hd:(hh + 1) * hd, :] = acc_scr[hh * hd:(hh + 1) * hd, :] / l_scr[hh:hh + 1, :]
    o_ref[0] = acc_scr[...].T.astype(BF16)


def _moba_prompt(q, kb, vb):
    b, l, w = q.shape
    blk = MOBA_BLOCK
    assert l % blk == 0
    nb = l // blk
    full = pl.BlockSpec((1, l, w), lambda i, j: (i, 0, 0))
    tile = pl.BlockSpec((1, blk, w), lambda i, j: (i, j, 0))
    return pl.pallas_call(
        _moba_prompt_kernel,
        grid=(b, nb),
        in_specs=[tile, full, full],
        out_specs=tile,
        out_shape=jax.ShapeDtypeStruct((b, l, w), BF16),
        scratch_shapes=[pltpu.VMEM((nb, w, blk), BF16), pltpu.VMEM((nb * MOBA_HEADS, w), BF16),
                        pltpu.VMEM((nb, MOBA_HEADS, blk), F32), pltpu.VMEM((w, blk), F32),
                        pltpu.VMEM((MOBA_HEADS, blk), F32), pltpu.VMEM((MOBA_HEADS, blk), F32),
                        pltpu.VMEM((MOBA_HEADS // 2, 2 * MOBA_HEAD_DIM, 2 * blk), BF16)],
        compiler_params=_cparams(("parallel", "arbitrary")),
        name="moba_prompt",
    )(q, kb, vb)


SAMPLE_GROUP = 4


def _moba_sample_kernel(layer, pt_ref, q_ref, kn_ref, vn_ref, ck_hbm, cv_hbm, o_ref,
                        kbuf, vbuf, ksem, vsem, m_scr, l_scr, g_scr, o_scr):
    b = pl.program_id(0)
    nb = pl.num_programs(0)
    n_pages = pt_ref.shape[1]
    page = kbuf.shape[3]
    ppb = MOBA_BLOCK // page
    n_full = n_pages // ppb
    grp = SAMPLE_GROUP
    n_grp = n_full // grp
    hd = MOBA_HEAD_DIM
    t = q_ref.shape[1]
    ncol = MOBA_HEADS * t
    scale = hd ** -0.5
    nt = (((1,), (1,)), ((), ()))

    def copies(bb, g, par):
        out = []
        for j in range(grp):
            for pg in range(ppb):
                pid = pt_ref[bb, (g * grp + j) * ppb + pg]
                out.append(pltpu.make_async_copy(ck_hbm.at[layer, pid], kbuf.at[par, j * ppb + pg], ksem.at[par]))
                out.append(pltpu.make_async_copy(cv_hbm.at[layer, pid], vbuf.at[par, j * ppb + pg], vsem.at[par]))
        return out

    @pl.when(b == 0)
    def _():
        for c in copies(0, 0, 0):
            c.start()

    q = q_ref[0]
    row_head = lax.broadcasted_iota(jnp.int32, (V7X_LANES, MOBA_WIDTH), 0) // t
    lane_head = lax.broadcasted_iota(jnp.int32, (V7X_LANES, MOBA_WIDTH), 1) // hd
    q_tiled = jnp.concatenate([q.astype(F32)] * (V7X_LANES // t), axis=0)
    qrows = jnp.where(row_head == lane_head, q_tiled, 0.0).astype(BF16)

    def packed(buf, par, j):
        return jnp.concatenate(
            [jnp.concatenate([buf[par, j * ppb + pg, hh] for hh in range(MOBA_HEADS)], axis=-1)
             for pg in range(ppb)], axis=0).astype(BF16)

    def group(g, carry):
        par = (b * n_grp + g) % 2

        @pl.when(g + 1 < n_grp)
        def _():
            for c in copies(b, g + 1, 1 - par):
                c.start()

        @pl.when(jnp.logical_and(g + 1 == n_grp, b + 1 < nb))
        def _():
            for c in copies(b + 1, 0, 1 - par):
                c.start()

        for c in copies(b, g, par):
            c.wait()
        for j in range(grp):
            n = g * grp + j
            s_raw = lax.dot_general(packed(kbuf, par, j), qrows, nt, preferred_element_type=F32)
            g_scr[pl.ds(n, 1), :] = jnp.sum(s_raw, axis=0, keepdims=True) * (1.0 / MOBA_BLOCK)
            s = s_raw * scale
            m = jnp.max(s, axis=0, keepdims=True)
            p = jnp.exp(s - m)
            m_scr[pl.ds(n, 1), :] = m
            l_scr[pl.ds(n, 1), :] = jnp.sum(p, axis=0, keepdims=True)
            o_scr[n] = jnp.dot(p.T[:ncol, :].astype(BF16), packed(vbuf, par, j), preferred_element_type=F32)
        return carry

    lax.fori_loop(0, n_grp, group, 0)

    gate = g_scr[...]
    blk_id = lax.broadcasted_iota(jnp.int32, gate.shape, 0)
    sel = jnp.zeros(gate.shape, jnp.bool_)
    for _ in range(min(MOBA_TOPK, n_full)):
        mx = jnp.max(gate, axis=0, keepdims=True)
        first = jnp.min(jnp.where(gate == mx, blk_id, n_full), axis=0, keepdims=True)
        hit = blk_id == first
        sel = jnp.logical_or(sel, hit)
        gate = jnp.where(hit, -jnp.inf, gate)

    s_new = lax.dot_general(kn_ref[0], qrows, nt, preferred_element_type=F32) * scale
    key_tok = lax.broadcasted_iota(jnp.int32, s_new.shape, 0)
    col_tok = lax.broadcasted_iota(jnp.int32, s_new.shape, 1) % t
    s_new = jnp.where(key_tok <= col_tok, s_new, -jnp.inf)
    m_blk = jnp.where(sel, m_scr[...], -jnp.inf)
    m_fin = jnp.maximum(jnp.max(m_blk, axis=0, keepdims=True), jnp.max(s_new, axis=0, keepdims=True))
    w_blk = jnp.where(sel, jnp.exp(m_blk - m_fin), 0.0)
    p_new = jnp.exp(s_new - m_fin)
    l_fin = jnp.sum(w_blk * l_scr[...], axis=0, keepdims=True) + jnp.sum(p_new, axis=0, keepdims=True)
    pad = jnp.zeros((V7X_LANES - n_full - t - 1, V7X_LANES), F32)
    cols = jnp.concatenate([w_blk, p_new, l_fin, pad], axis=0).T
    o_fin = jnp.dot(cols[:ncol, n_full:n_full + t].astype(BF16), vn_ref[0], preferred_element_type=F32)
    for n in range(n_full):
        o_fin = o_fin + cols[:ncol, n:n + 1] * o_scr[n]
    o_fin = o_fin / cols[:ncol, n_full + t:n_full + t + 1]
    for hh in range(MOBA_HEADS):
        o_ref[0, :, hh * hd:(hh + 1) * hd] = o_fin[hh * t:(hh + 1) * t, hh * hd:(hh + 1) * hd].astype(BF16)


def _moba_sample(q, kn, vn, cache_k, cache_v, page_table, layer):
    b, t, w = q.shape
    n_pages = page_table.shape[1]
    _, _, heads, page, hd = cache_k.shape
    ppb = MOBA_BLOCK // page
    assert MOBA_BLOCK % page == 0 and n_pages % ppb == 0 and heads == MOBA_HEADS and hd == MOBA_HEAD_DIM
    n_full = n_pages // ppb
    assert n_full % SAMPLE_GROUP == 0 and n_full + t + 1 <= V7X_LANES and heads * t <= V7X_LANES
    tok = pl.BlockSpec((1, t, w), lambda i, pt: (i, 0, 0))
    any_spec = pl.BlockSpec(memory_space=pl.ANY)
    pages = SAMPLE_GROUP * ppb
    grid_spec = pltpu.PrefetchScalarGridSpec(
        num_scalar_prefetch=1,
        grid=(b,),
        in_specs=[tok, tok, tok, any_spec, any_spec],
        out_specs=tok,
        scratch_shapes=[pltpu.VMEM((2, pages, heads, page, hd), F32), pltpu.VMEM((2, pages, heads, page, hd), F32),
                        pltpu.SemaphoreType.DMA((2,)), pltpu.SemaphoreType.DMA((2,)),
                        pltpu.VMEM((n_full, V7X_LANES), F32), pltpu.VMEM((n_full, V7X_LANES), F32),
                        pltpu.VMEM((n_full, V7X_LANES), F32), pltpu.VMEM((n_full, heads * t, w), F32)],
    )
    return pl.pallas_call(
        functools.partial(_moba_sample_kernel, layer),
        grid_spec=grid_spec,
        out_shape=jax.ShapeDtypeStruct((b, t, w), BF16),
        compiler_params=_cparams(("arbitrary",)),
        name="moba_sample",
    )(page_table, q, kn, vn, cache_k, cache_v)


def _merge_kernel(x_ref, c_ref, om_ref, omem_ref, g1_ref, wg_ref, wco_ref, bco_ref, wmo_ref, wmemo_ref, wout_ref,
                  g2_ref, wr_hi_ref, wr_lo_ref, br_ref, x1_out, h2_out, e_out, gate_out):
    tm, d = x_ref.shape
    x = x_ref[...]
    h = _rms(x, g1_ref[...]).astype(BF16)
    gates = jax.nn.sigmoid(jnp.dot(h, wg_ref[...], preferred_element_type=F32))
    u_conv = jnp.dot(c_ref[...], wco_ref[...], preferred_element_type=F32) + bco_ref[...]
    u_moba = jnp.dot(om_ref[...], wmo_ref[...], preferred_element_type=F32)
    u_mem = jnp.dot(omem_ref[...], wmemo_ref[...], preferred_element_type=F32)
    merged = gates[:, :d] * u_conv + gates[:, d:2 * d] * u_moba + gates[:, 2 * d:] * u_mem
    x1 = x + jnp.dot(merged.astype(BF16), wout_ref[...], preferred_element_type=F32)
    x1_out[...] = x1
    h2 = _rms(x1, g2_ref[...])
    for c in range(d // V7X_LANES):
        h2_out[:, c, :] = h2[:, c * V7X_LANES:(c + 1) * V7X_LANES]

    h2_hi = h2.astype(BF16)
    h2_lo = (h2 - h2_hi.astype(F32)).astype(BF16)
    nt = (((1,), (1,)), ((), ()))
    logits = (lax.dot_general(wr_hi_ref[...], h2_hi, nt, preferred_element_type=F32)
              + lax.dot_general(wr_hi_ref[...], h2_lo, nt, preferred_element_type=F32)
              + lax.dot_general(wr_lo_ref[...], h2_hi, nt, preferred_element_type=F32)) + br_ref[...]
    e_id = lax.broadcasted_iota(jnp.int32, logits.shape, 0)
    tops, ids = [], []
    for _ in range(TOP_K):
        mx = jnp.max(logits, axis=0, keepdims=True)
        first = jnp.min(jnp.where(logits == mx, e_id, N_EXPERTS), axis=0, keepdims=True)
        tops.append(mx)
        ids.append(first)
        logits = jnp.where(e_id == first, -jnp.inf, logits)
    ex = [jnp.exp(v - tops[0]) for v in tops]
    denom = ex[0] + ex[1] + ex[2] + ex[3]
    e_out[...] = jnp.concatenate(ids, axis=0)
    gate_rows = jnp.concatenate([v / denom for v in ex] + [jnp.zeros((V7X_LANES - TOP_K, tm), F32)], axis=0)
    gate_out[...] = gate_rows.T


def _merge(x2d, c, om, omem, tm, consts):
    n, d = x2d.shape
    assert n % tm == 0
    row = lambda w: pl.BlockSpec((tm, w), lambda i: (i, 0))
    n_chunk = d // V7X_LANES
    return pl.pallas_call(
        _merge_kernel,
        grid=(n // tm,),
        in_specs=[row(d), row(c.shape[1]), row(om.shape[1]), row(omem.shape[1])] + [_full(a.shape) for a in consts],
        out_specs=[row(d), pl.BlockSpec((tm, n_chunk, V7X_LANES), lambda i: (i, 0, 0)),
                   pl.BlockSpec((TOP_K, tm), lambda i: (0, i)), row(V7X_LANES)],
        out_shape=[jax.ShapeDtypeStruct((n, d), F32), jax.ShapeDtypeStruct((n, n_chunk, V7X_LANES), F32),
                   jax.ShapeDtypeStruct((TOP_K, n), jnp.int32), jax.ShapeDtypeStruct((n, V7X_LANES), F32)],
        compiler_params=_cparams(("parallel",)),
        name="merge",
    )(x2d, c, om, omem, *consts)


META_ROWS = V7X_SUBLANES
TOK_ROW0 = 0
DST_ROW0 = MOE_BLOCK // V7X_LANES


def _moe_kernel(be_ref, meta_hbm, h_hbm, wgu_ref, bgu_ref, wd_ref, bd_ref, o_hbm,
                meta_smem, msem, xbuf, xsem, ybuf, ysem):
    i = pl.program_id(0)
    n = pl.num_programs(0)
    rows = MOE_BLOCK
    n_chunk = xbuf.shape[2]

    def meta_copy(blk):
        return pltpu.make_async_copy(meta_hbm.at[pl.ds(pl.multiple_of(blk * META_ROWS, META_ROWS), META_ROWS)],
                                     meta_smem.at[blk % 3], msem.at[blk % 3])

    def gather_start(blk, slot):
        ms = blk % 3
        for r in range(rows):
            tok = meta_smem[ms, TOK_ROW0 + r // V7X_LANES, r % V7X_LANES]
            pltpu.make_async_copy(h_hbm.at[tok], xbuf.at[slot, r], xsem.at[slot]).start(priority=r % 2)

    def scatter_start(blk, slot, r0, r1):
        ms = blk % 3
        for r in range(r0, r1):
            dst = meta_smem[ms, DST_ROW0 + r // V7X_LANES, r % V7X_LANES]
            pltpu.make_async_copy(ybuf.at[slot, r], o_hbm.at[dst], ysem.at[slot]).start(priority=r % 2)

    def gather_wait(slot):
        pltpu.make_async_copy(h_hbm.at[pl.ds(0, rows)], xbuf.at[slot], xsem.at[slot]).wait()

    def scatter_wait(slot):
        pltpu.make_async_copy(ybuf.at[slot], o_hbm.at[pl.ds(0, rows)], ysem.at[slot]).wait()

    slot = i % 2

    @pl.when(i == 0)
    def _():
        meta_copy(0).start()
        meta_copy(0).wait()
        gather_start(0, 0)

        @pl.when(n > 1)
        def _():
            meta_copy(1).start()

    @pl.when(i + 2 < n)
    def _():
        meta_copy(i + 2).start()

    @pl.when(i + 1 < n)
    def _():
        meta_copy(i + 1).wait()

    @pl.when(i >= 2)
    def _():
        scatter_wait(slot)

    gather_start(jnp.minimum(i + 1, n - 1), 1 - slot)
    gather_wait(slot)
    x = jnp.concatenate([xbuf[slot, :, c, :] for c in range(n_chunk)], axis=-1).astype(BF16)
    gu = jnp.dot(x, wgu_ref[0], preferred_element_type=F32) + bgu_ref[0]
    dff = gu.shape[-1] // 2
    g = jnp.minimum(gu[:, :dff], SWIGLU_LIMIT)
    u = jnp.clip(gu[:, dff:], -SWIGLU_LIMIT, SWIGLU_LIMIT)
    act = (g * jax.nn.sigmoid(SWIGLU_ALPHA * g) * (u + 1.0)).astype(BF16)
    y = jnp.dot(act, wd_ref[0], preferred_element_type=F32) + bd_ref[0]
    for c in range(n_chunk):
        ybuf[slot, :, c, :] = y[:, c * V7X_LANES:(c + 1) * V7X_LANES]
    scatter_start(i, slot, 0, rows)

    @pl.when(i == n - 1)
    def _():
        gather_wait(1 - slot)
        scatter_wait(slot)

        @pl.when(n > 1)
        def _():
            scatter_wait(1 - slot)


def _moe(block_e, meta, h_rows, n_out_rows, wgu, bgu, wd, bd):
    n_blocks = block_e.shape[0]
    _, n_chunk, lanes = h_rows.shape
    d = n_chunk * lanes
    any_spec = pl.BlockSpec(memory_space=pl.ANY)
    grid_spec = pltpu.PrefetchScalarGridSpec(
        num_scalar_prefetch=1,
        grid=(n_blocks,),
        in_specs=[any_spec, any_spec,
                  pl.BlockSpec((1, d, wgu.shape[2]), lambda i, be: (be[i], 0, 0)),
                  pl.BlockSpec((1, 1, bgu.shape[2]), lambda i, be: (be[i], 0, 0)),
                  pl.BlockSpec((1, wd.shape[1], d), lambda i, be: (be[i], 0, 0)),
                  pl.BlockSpec((1, 1, d), lambda i, be: (be[i], 0, 0))],
        out_specs=any_spec,
        scratch_shapes=[pltpu.SMEM((3, META_ROWS, V7X_LANES), jnp.int32), pltpu.SemaphoreType.DMA((3,)),
                        pltpu.VMEM((2, MOE_BLOCK, n_chunk, lanes), F32), pltpu.SemaphoreType.DMA((2,)),
                        pltpu.VMEM((2, MOE_BLOCK, n_chunk, lanes), F32), pltpu.SemaphoreType.DMA((2,))],
    )
    return pl.pallas_call(
        _moe_kernel,
        grid_spec=grid_spec,
        out_shape=jax.ShapeDtypeStruct((n_out_rows, n_chunk, lanes), F32),
        compiler_params=_cparams(("arbitrary",)),
        name="moe",
    )(block_e, meta, h_rows, wgu, bgu, wd, bd)


def _route(top_e, n_tok):
    nk = n_tok * TOP_K
    flat_e = top_e.reshape(nk)
    order = jnp.argsort(flat_e).astype(jnp.int32)
    e_sorted = flat_e[order]
    counts = jnp.bincount(flat_e, length=N_EXPERTS).astype(jnp.int32)
    padded = (counts + MOE_BLOCK - 1) // MOE_BLOCK * MOE_BLOCK
    pad_end = jnp.cumsum(padded)
    pad_start = pad_end - padded
    grp_start = jnp.cumsum(counts) - counts
    dest = pad_start[e_sorted] + jnp.arange(nk, dtype=jnp.int32) - grp_start[e_sorted]
    n_blocks = (nk + N_EXPERTS * (MOE_BLOCK - 1) + MOE_BLOCK - 1) // MOE_BLOCK
    n_rows = n_blocks * MOE_BLOCK
    row_asg = jnp.full((n_rows,), -1, jnp.int32).at[dest].set(order, unique_indices=True)
    is_real = row_asg >= 0
    row_tok = jnp.where(is_real, row_asg % n_tok, 0)
    spill = nk + jnp.cumsum(jnp.logical_not(is_real).astype(jnp.int32)) - 1
    row_dst = jnp.where(is_real, row_asg, spill)
    block_e = jnp.minimum(jnp.searchsorted(pad_end, jnp.arange(n_blocks, dtype=jnp.int32) * MOE_BLOCK, side="right"),
                          N_EXPERTS - 1).astype(jnp.int32)
    per = MOE_BLOCK // V7X_LANES
    meta = jnp.concatenate([row_tok.reshape(n_blocks, per, V7X_LANES), row_dst.reshape(n_blocks, per, V7X_LANES),
                            jnp.zeros((n_blocks, META_ROWS - 2 * per, V7X_LANES), jnp.int32)], axis=1)
    return block_e, meta.reshape(n_blocks * META_ROWS, V7X_LANES), n_rows


def _combine_kernel(x1_ref, gate_ref, o0_ref, o1_ref, o2_ref, o3_ref, y_ref):
    gate = gate_ref[...]
    n_chunk = o0_ref.shape[1]
    cols = [gate[:, k:k + 1] for k in range(TOP_K)]
    for c in range(n_chunk):
        sl = slice(c * V7X_LANES, (c + 1) * V7X_LANES)
        acc = x1_ref[:, sl]
        for k, o_ref in enumerate((o0_ref, o1_ref, o2_ref, o3_ref)):
            acc = acc + cols[k] * o_ref[:, c, :]
        y_ref[:, sl] = acc


def _combine(x1, gate_col, o_rows, tm):
    n, d = x1.shape
    _, n_chunk, lanes = o_rows.shape
    assert n % tm == 0
    per = n // tm
    o_spec = lambda k: pl.BlockSpec((tm, n_chunk, lanes), lambda i, k=k: (k * per + i, 0, 0))
    return pl.pallas_call(
        _combine_kernel,
        grid=(per,),
        in_specs=[pl.BlockSpec((tm, d), lambda i: (i, 0)), pl.BlockSpec((tm, V7X_LANES), lambda i: (i, 0))]
                 + [o_spec(k) for k in range(TOP_K)],
        out_specs=pl.BlockSpec((tm, d), lambda i: (i, 0)),
        out_shape=jax.ShapeDtypeStruct((n, d), F32),
        compiler_params=_cparams(("parallel",)),
        name="combine",
    )(x1, gate_col, o_rows, o_rows, o_rows, o_rows)


def _prep_weights(p):
    cw = p["w_dw"].shape[-1]
    o_gate = 2 * cw + 3 * MOBA_WIDTH + MEM_WIDTH
    row = lambda a: a.reshape(1, -1).astype(F32)
    tile = lambda g, n: jnp.tile(g.astype(F32), n).reshape(1, -1)
    prep = {}
    prep["proj"] = (
        row(p["g_norm1"]), p["w_in"][:, :o_gate].astype(BF16),
        _segment_ones(MOBA_WIDTH, MOBA_HEAD_DIM), _segment_ones(MEM_WIDTH, MEM_HEAD_DIM),
        tile(p["g_q_moba"], MOBA_HEADS), tile(p["g_k_moba"], MOBA_HEADS), tile(p["g_q_mem"], MEM_HEADS),
        p["w_dw"].astype(F32), row(p["b_dw"]), row(p["g_conv_ln"]), row(p["b_conv_ln"]),
    )
    prep["mem_kv"] = (row(p["g_mem_norm"]), p["w_mem_kv"].astype(BF16), _segment_ones(MEM_WIDTH, MEM_HEAD_DIM),
                      tile(p["g_k_mem"], MEM_HEADS))
    wr_t = p["w_router"].astype(F32).T
    wr_hi = wr_t.astype(BF16)
    wr_lo = (wr_t - wr_hi.astype(F32)).astype(BF16)
    prep["merge"] = (
        row(p["g_norm1"]), p["w_in"][:, o_gate:].astype(BF16), p["w_conv_out"].astype(BF16), row(p["b_conv_out"]),
        p["w_moba_o"].astype(BF16), p["w_mem_o"].astype(BF16), p["w_out"].astype(BF16), row(p["g_norm2"]),
        wr_hi, wr_lo, p["b_router"].astype(F32).reshape(-1, 1),
    )
    prep["moe"] = (p["w_gu"].astype(BF16), p["b_gu"].astype(F32)[:, None, :], p["w_down"].astype(BF16),
                   p["b_down"].astype(F32)[:, None, :])
    return prep


def _ffn(x2d, c, om, omem, tm, prep):
    n = x2d.shape[0]
    x1, h_rows, top_e, gate_col = _merge(x2d, c, om, omem, tm, prep["merge"])
    block_e, meta, n_rows = _route(top_e, n)
    o_rows = _moe(block_e, meta, h_rows, n_rows, *prep["moe"])
    return _combine(x1, gate_col, o_rows, tm)


_PARAM_NAMES = ("g_norm1", "w_in", "w_dw", "b_dw", "g_conv_ln", "b_conv_ln", "w_conv_out", "b_conv_out", "g_q_moba",
                "g_k_moba", "w_moba_o", "g_mem_norm", "w_mem_kv", "g_q_mem", "g_k_mem", "w_mem_o", "w_out", "g_norm2",
                "w_router", "b_router", "w_gu", "b_gu", "w_down", "b_down")

PROMPT_ROWS = 512
SAMPLE_BATCH_TILE = 64
SAMPLE_MEM_TILE = 8


def kernel(x_prompt, x_sample, mem_prompt, cache_moba_k, cache_moba_v, cache_mem_k, cache_mem_v, state_conv, page_table, g_norm1, w_in, w_dw, b_dw, g_conv_ln, b_conv_ln, w_conv_out, b_conv_out, g_q_moba, g_k_moba, w_moba_o, g_mem_norm, w_mem_kv, g_q_mem, g_k_mem, w_mem_o, w_out, g_norm2, w_router, b_router, w_gu, b_gu, w_down, b_down):
    params = dict(zip(_PARAM_NAMES, (g_norm1, w_in, w_dw, b_dw, g_conv_ln, b_conv_ln, w_conv_out, b_conv_out,
                                     g_q_moba, g_k_moba, w_moba_o, g_mem_norm, w_mem_kv, g_q_mem, g_k_mem, w_mem_o,
                                     w_out, g_norm2, w_router, b_router, w_gu, b_gu, w_down, b_down)))
    depth = g_norm1.shape[0]
    bp, lp, d = x_prompt.shape
    bs, ls, _ = x_sample.shape
    n_mem = mem_prompt.shape[1]
    y_p, y_s = x_prompt, x_sample
    outs = [[] for _ in range(8)]
    for layer in range(depth):
        prep = _prep_weights({k: v[layer] for k, v in params.items()})
        k, v, q, kb, vb, qm, c, tail = _proj(y_p, None, 1, PROMPT_ROWS, *prep["proj"])
        mk, mv = _mem_kv(mem_prompt.reshape(bp * n_mem, d), PROMPT_ROWS, *prep["mem_kv"])
        o_moba = _moba_prompt(q, kb, vb)
        o_mem = _mem_attend(qm, mk.reshape(1, bp, n_mem, MEM_WIDTH), mv.reshape(1, bp, n_mem, MEM_WIDTH), 0, 1,
                            PROMPT_ROWS)
        flat = lambda a: a.reshape(bp * lp, a.shape[-1])
        y_p = _ffn(flat(y_p), flat(c), flat(o_moba), flat(o_mem), PROMPT_ROWS, prep).reshape(bp, lp, d)
        mem_shape = (bp, n_mem, MEM_HEADS, MEM_HEAD_DIM)
        for lst, val in zip(outs[:5], (k, v, mk.reshape(mem_shape), mv.reshape(mem_shape), tail)):
            lst.append(val)
        k, v, q, kb, vb, qm, c, tail = _proj(y_s, state_conv[layer], SAMPLE_BATCH_TILE, ls, *prep["proj"])
        o_moba = _moba_sample(q, kb, vb, cache_moba_k, cache_moba_v, page_table, layer)
        o_mem = _mem_attend(qm, cache_mem_k.reshape(depth, bs, n_mem, MEM_WIDTH),
                            cache_mem_v.reshape(depth, bs, n_mem, MEM_WIDTH), layer, SAMPLE_MEM_TILE, ls)
        flat = lambda a: a.reshape(bs * ls, a.shape[-1])
        y_s = _ffn(flat(y_s), flat(c), flat(o_moba), flat(o_mem), PROMPT_ROWS, prep).reshape(bs, ls, d)
        for lst, val in zip(outs[5:], (k, v, tail)):
            lst.append(val)
    kp, vp, mkp, mvp, cp, ks, vs, cs = (jnp.stack(o) for o in outs)
    return (y_p, y_s, kp, vp, mkp, mvp, cp, ks, vs, cs)
```

```python
import functools

import jax
import jax.numpy as jnp
from jax import lax
from jax.experimental import pallas as pl
from jax.experimental.pallas import tpu as pltpu

F32 = jnp.float32
BF16 = jnp.bfloat16

EPS = 1e-6
MOBA_HEADS = 8
MOBA_HEAD_DIM = 64
MOBA_WIDTH = MOBA_HEADS * MOBA_HEAD_DIM
MOBA_BLOCK = 256
MOBA_TOPK = 3
MEM_HEADS = 4
MEM_HEAD_DIM = 128
MEM_WIDTH = MEM_HEADS * MEM_HEAD_DIM
CONV_KERNEL = 31
CONV_HIST = CONV_KERNEL - 1
N_EXPERTS = 32
TOP_K = 4
SWIGLU_LIMIT = 7.0
SWIGLU_ALPHA = 1.702
MOE_BLOCK = 256

V7X_LANES = 128
V7X_SUBLANES = 8
HIST_PAD = 32
VMEM_LIMIT = 56 * 1024 * 1024


def _cparams(sem):
    return pltpu.CompilerParams(dimension_semantics=sem, vmem_limit_bytes=VMEM_LIMIT)


def _full(shape):
    n = len(shape)
    return pl.BlockSpec(shape, lambda *_: (0,) * n)


def _segment_ones(width, seg):
    r = jnp.arange(width) // seg
    return (r[:, None] == r[None, :]).astype(BF16)


def _rms(x, g):
    return x * lax.rsqrt(jnp.mean(x * x, axis=-1, keepdims=True) + EPS) * g


def _seg_rms(z, seg_ones, seg, g):
    sq = (z * z).astype(BF16)
    parts = []
    for c in range(0, z.shape[-1], 2 * V7X_LANES):
        parts.append(jnp.dot(sq[:, c:c + 2 * V7X_LANES], seg_ones[c:c + 2 * V7X_LANES, c:c + 2 * V7X_LANES],
                             preferred_element_type=F32))
    ms = jnp.concatenate(parts, axis=-1) * (1.0 / seg)
    return z * lax.rsqrt(ms + EPS) * g


def _proj_kernel(has_hist, carry, x_ref, *refs):
    if has_hist:
        hist_ref, refs = refs[0], refs[1:]
    (g1_ref, w1_ref, seg64_ref, seg128_ref, gq_ref, gk_ref, gqm_ref, wdw_ref, bdw_ref, gln_ref, bln_ref,
     k_out, v_out, q_out, kb_out, vb_out, qm_out, c_out, tail_out, xp_scr) = refs
    bt, tl, d = x_ref.shape
    rows = bt * tl
    cw = c_out.shape[-1]
    t = pl.program_id(1)

    x = x_ref[...].reshape(rows, d)
    h = _rms(x, g1_ref[...]).astype(BF16)
    z = jnp.dot(h, w1_ref[...], preferred_element_type=F32)
    o_q = 2 * cw
    o_k = o_q + MOBA_WIDTH
    o_v = o_k + MOBA_WIDTH
    o_qm = o_v + MOBA_WIDTH
    u = z[:, :cw] * jax.nn.sigmoid(z[:, cw:o_q])
    seg64 = seg64_ref[...]
    qn = _seg_rms(z[:, o_q:o_k], seg64, MOBA_HEAD_DIM, gq_ref[...])
    kn = _seg_rms(z[:, o_k:o_v], seg64, MOBA_HEAD_DIM, gk_ref[...])
    vv = z[:, o_v:o_qm]
    qmn = _seg_rms(z[:, o_qm:o_qm + MEM_WIDTH], seg128_ref[...], MEM_HEAD_DIM, gqm_ref[...])

    q_out[...] = qn.astype(BF16).reshape(bt, tl, MOBA_WIDTH)
    kb_out[...] = kn.astype(BF16).reshape(bt, tl, MOBA_WIDTH)
    vb_out[...] = vv.astype(BF16).reshape(bt, tl, MOBA_WIDTH)
    qm_out[...] = qmn.astype(BF16).reshape(bt, tl, MEM_WIDTH)
    kn3 = kn.reshape(bt, tl, MOBA_WIDTH)
    vv3 = vv.reshape(bt, tl, MOBA_WIDTH)
    for hh in range(MOBA_HEADS):
        sl = slice(hh * MOBA_HEAD_DIM, (hh + 1) * MOBA_HEAD_DIM)
        k_out[:, hh, :, :] = kn3[:, :, sl]
        v_out[:, hh, :, :] = vv3[:, :, sl]

    lo = HIST_PAD - CONV_HIST

    @pl.when(t == 0)
    def _():
        xp_scr[:, 0:HIST_PAD, :] = jnp.zeros((bt, HIST_PAD, cw), F32)
        if has_hist:
            xp_scr[:, lo:HIST_PAD, :] = hist_ref[...]

    xp_scr[:, HIST_PAD:HIST_PAD + tl, :] = u.reshape(bt, tl, cw)
    rc = min(tl, 64)
    bdw = bdw_ref[...]
    gln = gln_ref[...]
    bln = bln_ref[...]
    for r0 in range(0, tl, rc):
        window = xp_scr[:, r0:r0 + rc + HIST_PAD, :]
        acc = jnp.zeros((bt, rc, cw), F32) + bdw
        for s in range(V7X_SUBLANES):
            ext = rc if s == 0 else rc + V7X_SUBLANES
            part = None
            for q in range(HIST_PAD // V7X_SUBLANES + 1):
                j = V7X_SUBLANES * q + s - lo
                if 0 <= j < CONV_KERNEL:
                    term = wdw_ref[j:j + 1, :] * window[:, V7X_SUBLANES * q:V7X_SUBLANES * q + ext, :]
                    part = term if part is None else part + term
            acc = acc + (part if s == 0 else part[:, s:s + rc, :])
        mu = jnp.mean(acc, axis=-1, keepdims=True)
        dev = acc - mu
        var = jnp.mean(dev * dev, axis=-1, keepdims=True)
        y = dev * lax.rsqrt(var + EPS) * gln + bln
        c_out[:, r0:r0 + rc, :] = (y * jax.nn.sigmoid(y)).astype(BF16)
    tail_out[...] = xp_scr[:, tl + lo:tl + HIST_PAD, :]
    if carry:
        xp_scr[:, 0:HIST_PAD, :] = xp_scr[:, tl:tl + HIST_PAD, :]


def _proj(x, hist, bt, tl, g1, w1, seg64, seg128, gq, gk, gqm, wdw, bdw, gln, bln):
    b, l, d = x.shape
    cw = wdw.shape[-1]
    n_t = l // tl
    has_hist = hist is not None
    carry = n_t > 1
    assert b % bt == 0 and l % tl == 0 and (not carry or (tl >= HIST_PAD and bt == 1))
    tok = lambda w: pl.BlockSpec((bt, tl, w), lambda i, j: (i, j, 0))
    head = pl.BlockSpec((bt, MOBA_HEADS, tl, MOBA_HEAD_DIM), lambda i, j: (i, 0, j, 0))
    in_specs = [tok(d)]
    args = [x]
    if has_hist:
        in_specs.append(pl.BlockSpec((bt, CONV_HIST, cw), lambda i, j: (i, 0, 0)))
        args.append(hist)
    consts = [g1, w1, seg64, seg128, gq, gk, gqm, wdw, bdw, gln, bln]
    in_specs += [_full(c.shape) for c in consts]
    out_shape = [
        jax.ShapeDtypeStruct((b, MOBA_HEADS, l, MOBA_HEAD_DIM), F32),
        jax.ShapeDtypeStruct((b, MOBA_HEADS, l, MOBA_HEAD_DIM), F32),
        jax.ShapeDtypeStruct((b, l, MOBA_WIDTH), BF16),
        jax.ShapeDtypeStruct((b, l, MOBA_WIDTH), BF16),
        jax.ShapeDtypeStruct((b, l, MOBA_WIDTH), BF16),
        jax.ShapeDtypeStruct((b, l, MEM_WIDTH), BF16),
        jax.ShapeDtypeStruct((b, l, cw), BF16),
        jax.ShapeDtypeStruct((b, CONV_HIST, cw), F32),
    ]
    out_specs = [head, head, tok(MOBA_WIDTH), tok(MOBA_WIDTH), tok(MOBA_WIDTH), tok(MEM_WIDTH), tok(cw),
                 pl.BlockSpec((bt, CONV_HIST, cw), lambda i, j: (i, 0, 0))]
    return pl.pallas_call(
        functools.partial(_proj_kernel, has_hist, carry),
        grid=(b // bt, n_t),
        in_specs=in_specs,
        out_specs=out_specs,
        out_shape=out_shape,
        scratch_shapes=[pltpu.VMEM((bt, HIST_PAD + tl, cw), F32)],
        compiler_params=_cparams(("parallel", "arbitrary")),
        name="proj",
    )(*args, *consts)


def _memkv_kernel(x_ref, g_ref, w_ref, seg128_ref, gk_ref, k_out, v_out):
    h = _rms(x_ref[...], g_ref[...]).astype(BF16)
    kv = jnp.dot(h, w_ref[...], preferred_element_type=F32)
    k_out[...] = _seg_rms(kv[:, :MEM_WIDTH], seg128_ref[...], MEM_HEAD_DIM, gk_ref[...])
    v_out[...] = kv[:, MEM_WIDTH:]


def _mem_kv(mem2d, tm, g, w, seg128, gk):
    n, d = mem2d.shape
    assert n % tm == 0
    spec = pl.BlockSpec((tm, MEM_WIDTH), lambda i: (i, 0))
    return pl.pallas_call(
        _memkv_kernel,
        grid=(n // tm,),
        in_specs=[pl.BlockSpec((tm, d), lambda i: (i, 0))] + [_full(c.shape) for c in (g, w, seg128, gk)],
        out_specs=[spec, spec],
        out_shape=[jax.ShapeDtypeStruct((n, MEM_WIDTH), F32)] * 2,
        compiler_params=_cparams(("parallel",)),
        name="mem_kv",
    )(mem2d, g, w, seg128, gk)


def _mematt_kernel(q_ref, k_ref, v_ref, o_ref):
    scale = MEM_HEAD_DIM ** -0.5
    for hh in range(MEM_HEADS):
        sl = slice(hh * MEM_HEAD_DIM, (hh + 1) * MEM_HEAD_DIM)
        q = q_ref[:, :, sl]
        k = k_ref[0, :, :, sl].astype(BF16)
        v = v_ref[0, :, :, sl].astype(BF16)
        s = jnp.einsum("bqd,bkd->bqk", q, k, preferred_element_type=F32) * scale
        m = jnp.max(s, axis=-1, keepdims=True)
        p = jnp.exp(s - m)
        l = jnp.sum(p, axis=-1, keepdims=True)
        o = jnp.einsum("bqk,bkd->bqd", p.astype(BF16), v, preferred_element_type=F32)
        o_ref[:, :, sl] = (o / l).astype(BF16)


def _mem_attend(qm, mk, mv, layer, bt, tq):
    b, l, w = qm.shape
    m = mk.shape[2]
    assert b % bt == 0 and l % tq == 0
    return pl.pallas_call(
        _mematt_kernel,
        grid=(b // bt, l // tq),
        in_specs=[pl.BlockSpec((bt, tq, w), lambda i, j: (i, j, 0)),
                  pl.BlockSpec((1, bt, m, w), lambda i, j: (layer, i, 0, 0)),
                  pl.BlockSpec((1, bt, m, w), lambda i, j: (layer, i, 0, 0))],
        out_specs=pl.BlockSpec((bt, tq, w), lambda i, j: (i, j, 0)),
        out_shape=jax.ShapeDtypeStruct((b, l, w), BF16),
        compiler_params=_cparams(("parallel", "arbitrary")),
        name="mem_attend",
    )(qm, mk, mv)


def _moba_prompt_kernel(q_ref, k_ref, v_ref, o_ref, vt_scr, km_scr, sel_scr, acc_scr, m_scr, l_scr, wq_scr):
    blk = MOBA_BLOCK
    l = k_ref.shape[1]
    nb = l // blk
    qb = pl.program_id(1)
    hd = MOBA_HEAD_DIM

    @pl.when(qb == 0)
    def _():
        lane_head = lax.broadcasted_iota(jnp.int32, (MOBA_HEADS, MOBA_WIDTH), 1) // hd
        row_head = lax.broadcasted_iota(jnp.int32, (MOBA_HEADS, MOBA_WIDTH), 0)
        head_mask = (lane_head == row_head).astype(F32)
        for n in range(nb):
            vt_scr[n] = v_ref[0, n * blk:(n + 1) * blk, :].T
            kmean = jnp.mean(k_ref[0, n * blk:(n + 1) * blk, :].astype(F32), axis=0, keepdims=True)
            km_scr[n * MOBA_HEADS:(n + 1) * MOBA_HEADS, :] = (kmean * head_mask).astype(BF16)

    qt = q_ref[0].T
    gate = jnp.dot(km_scr[...], qt, preferred_element_type=F32).reshape(nb, MOBA_HEADS, blk)
    for n in range(nb):
        rank = jnp.zeros((MOBA_HEADS, blk), jnp.int32)
        for n2 in range(nb):
            if n2 == n:
                continue
            beats = (gate[n2] >= gate[n]) if n2 < n else (gate[n2] > gate[n])
            rank = rank + jnp.where(jnp.logical_and(beats, n2 < qb), 1, 0)
        sel_scr[n] = jnp.where(jnp.logical_and(rank < MOBA_TOPK, n < qb), 1.0, 0.0)

    pairs = MOBA_HEADS // 2
    row_in_pair = lax.broadcasted_iota(jnp.int32, (2 * hd, blk), 0) // hd
    for pr in range(pairs):
        qp = qt[pr * 2 * hd:(pr + 1) * 2 * hd, :] * jnp.asarray(hd ** -0.5, BF16)
        zero = jnp.zeros_like(qp)
        wq_scr[pr] = jnp.concatenate([jnp.where(row_in_pair == 0, qp, zero), jnp.where(row_in_pair == 1, qp, zero)],
                                     axis=1)

    def attend(kblk, n, keep, first):
        s_pairs = [jnp.dot(kblk[:, pr * 2 * hd:(pr + 1) * 2 * hd], wq_scr[pr], preferred_element_type=F32)
                   for pr in range(pairs)]
        m_old = None if first else m_scr[...]
        l_old = None if first else l_scr[...]
        m_rows, l_rows, alphas, probs = [], [], [], []
        for hh in range(MOBA_HEADS):
            s = jnp.where(keep(hh), s_pairs[hh // 2][:, (hh % 2) * blk:(hh % 2 + 1) * blk], -jnp.inf)
            m_new = jnp.max(s, axis=0, keepdims=True)
            if not first:
                m_new = jnp.maximum(m_old[hh:hh + 1, :], m_new)
                alphas.append(jnp.exp(m_old[hh:hh + 1, :] - m_new))
            p = jnp.exp(s - m_new)
            p_sum = jnp.sum(p, axis=0, keepdims=True)
            l_rows.append(p_sum if first else alphas[hh] * l_old[hh:hh + 1, :] + p_sum)
            m_rows.append(m_new)
            probs.append(p.astype(BF16))
        m_scr[...] = jnp.concatenate(m_rows, axis=0)
        l_scr[...] = jnp.concatenate(l_rows, axis=0)
        pvs = [jnp.dot(vt_scr[n, hh * hd:(hh + 1) * hd, :], probs[hh], preferred_element_type=F32)
               for hh in range(MOBA_HEADS)]
        for hh in range(MOBA_HEADS):
            rows = slice(hh * hd, (hh + 1) * hd)
            acc_scr[rows, :] = pvs[hh] if first else alphas[hh] * acc_scr[rows, :] + pvs[hh]

    causal = lax.broadcasted_iota(jnp.int32, (blk, blk), 0) <= lax.broadcasted_iota(jnp.int32, (blk, blk), 1)
    attend(k_ref[0, pl.ds(pl.multiple_of(qb * blk, blk), blk), :], qb, lambda hh: causal, True)

    def past_block(n, carry):
        sel = sel_scr[n]
        attend(k_ref[0, pl.ds(pl.multiple_of(n * blk, blk), blk), :], n, lambda hh: sel[hh:hh + 1, :] > 0.0, False)
        return carry

    lax.fori_loop(0, qb, past_block, 0)

    for hh in range(MOBA_HEADS):
        acc_scr[hh * hd:(hh + 1) * hd, :] = acc_scr[hh * hd:(hh + 1) * hd, :] / l_scr[hh:hh + 1, :]
    o_ref[0] = acc_scr[...].T.astype(BF16)


def _moba_prompt(q, kb, vb):
    b, l, w = q.shape
    blk = MOBA_BLOCK
    assert l % blk == 0
    nb = l // blk
    full = pl.BlockSpec((1, l, w), lambda i, j: (i, 0, 0))
    tile = pl.BlockSpec((1, blk, w), lambda i, j: (i, j, 0))
    return pl.pallas_call(
        _moba_prompt_kernel,
        grid=(b, nb),
        in_specs=[tile, full, full],
        out_specs=tile,
        out_shape=jax.ShapeDtypeStruct((b, l, w), BF16),
        scratch_shapes=[pltpu.VMEM((nb, w, blk), BF16), pltpu.VMEM((nb * MOBA_HEADS, w), BF16),
                        pltpu.VMEM((nb, MOBA_HEADS, blk), F32), pltpu.VMEM((w, blk), F32),
                        pltpu.VMEM((MOBA_HEADS, blk), F32), pltpu.VMEM((MOBA_HEADS, blk), F32),
                        pltpu.VMEM((MOBA_HEADS // 2, 2 * MOBA_HEAD_DIM, 2 * blk), BF16)],
        compiler_params=_cparams(("parallel", "arbitrary")),
        name="moba_prompt",
    )(q, kb, vb)


SAMPLE_GROUP = 4


def _moba_sample_kernel(layer, pt_ref, q_ref, kn_ref, vn_ref, ck_hbm, cv_hbm, o_ref,
                        kbuf, vbuf, ksem, vsem, m_scr, l_scr, g_scr, o_scr):
    b = pl.program_id(0)
    nb = pl.num_programs(0)
    n_pages = pt_ref.shape[1]
    page = kbuf.shape[4]
    ppb = MOBA_BLOCK // page
    n_full = n_pages // ppb
    grp = SAMPLE_GROUP
    n_grp = n_full // grp
    hd = MOBA_HEAD_DIM
    t = q_ref.shape[1]
    ncol = MOBA_HEADS * t
    scale = hd ** -0.5
    nt = (((1,), (1,)), ((), ()))

    def copies(bb, g, par):
        out = []
        for j in range(grp):
            for pg in range(ppb):
                pid = pt_ref[bb, (g * grp + j) * ppb + pg]
                out.append(pltpu.make_async_copy(ck_hbm.at[layer, pid], kbuf.at[par, j * ppb + pg], ksem.at[par]))
                out.append(pltpu.make_async_copy(cv_hbm.at[layer, pid], vbuf.at[par, j * ppb + pg], vsem.at[par]))
        return out

    @pl.when(b == 0)
    def _():
        for c in copies(0, 0, 0):
            c.start()

    q = q_ref[0]
    row_head = lax.broadcasted_iota(jnp.int32, (V7X_LANES, MOBA_WIDTH), 0) // t
    lane_head = lax.broadcasted_iota(jnp.int32, (V7X_LANES, MOBA_WIDTH), 1) // hd
    q_tiled = jnp.concatenate([q.astype(F32)] * (V7X_LANES // t), axis=0)
    qrows = jnp.where(row_head == lane_head, q_tiled, 0.0).astype(BF16)

    def packed(buf, par, j):
        return jnp.concatenate([buf[par, j * ppb + pg].reshape(MOBA_WIDTH, page) for pg in range(ppb)],
                               axis=-1).astype(BF16)

    lane_id = lax.broadcasted_iota(jnp.int32, (V7X_LANES, V7X_LANES), 1)
    g_scr[...] = jnp.full((V7X_LANES, V7X_LANES), -jnp.inf, F32)
    m_scr[...] = jnp.full((V7X_LANES, V7X_LANES), -jnp.inf, F32)
    l_scr[...] = jnp.zeros((V7X_LANES, V7X_LANES), F32)

    def group(g, carry):
        par = (b * n_grp + g) % 2

        @pl.when(g + 1 < n_grp)
        def _():
            for c in copies(b, g + 1, 1 - par):
                c.start()

        @pl.when(jnp.logical_and(g + 1 == n_grp, b + 1 < nb))
        def _():
            for c in copies(b + 1, 0, 1 - par):
                c.start()

        for c in copies(b, g, par):
            c.wait()
        for j in range(grp):
            n = g * grp + j
            s_raw = jnp.dot(qrows, packed(kbuf, par, j), preferred_element_type=F32)
            s = s_raw * scale
            m = jnp.max(s, axis=-1, keepdims=True)
            p = jnp.exp(s - m)
            hit = lane_id == n
            g_scr[...] = jnp.where(hit, jnp.sum(s_raw, axis=-1, keepdims=True) * (1.0 / MOBA_BLOCK), g_scr[...])
            m_scr[...] = jnp.where(hit, m, m_scr[...])
            l_scr[...] = jnp.where(hit, jnp.sum(p, axis=-1, keepdims=True), l_scr[...])
            o_scr[n] = lax.dot_general(p[:ncol, :].astype(BF16), packed(vbuf, par, j), nt, preferred_element_type=F32)
        return carry

    lax.fori_loop(0, n_grp, group, 0)

    gate = g_scr[...]
    sel = jnp.zeros(gate.shape, jnp.bool_)
    for _ in range(min(MOBA_TOPK, n_full)):
        mx = jnp.max(gate, axis=-1, keepdims=True)
        first = jnp.min(jnp.where(gate == mx, lane_id, V7X_LANES), axis=-1, keepdims=True)
        hit = lane_id == first
        sel = jnp.logical_or(sel, hit)
        gate = jnp.where(hit, -jnp.inf, gate)

    s_new = lax.dot_general(qrows, kn_ref[0], nt, preferred_element_type=F32) * scale
    row_tok = lax.broadcasted_iota(jnp.int32, s_new.shape, 0) % t
    key_tok = lax.broadcasted_iota(jnp.int32, s_new.shape, 1)
    s_new = jnp.where(key_tok <= row_tok, s_new, -jnp.inf)
    m_blk = jnp.where(sel, m_scr[...], -jnp.inf)
    m_fin = jnp.maximum(jnp.max(m_blk, axis=-1, keepdims=True), jnp.max(s_new, axis=-1, keepdims=True))
    w_blk = jnp.where(sel, jnp.exp(m_blk - m_fin), 0.0)
    p_new = jnp.exp(s_new - m_fin)
    l_fin = jnp.sum(w_blk * l_scr[...], axis=-1, keepdims=True) + jnp.sum(p_new, axis=-1, keepdims=True)
    o_fin = jnp.dot(p_new[:ncol, :].astype(BF16), vn_ref[0], preferred_element_type=F32)
    for n in range(n_full):
        o_fin = o_fin + w_blk[:ncol, n:n + 1] * o_scr[n]
    o_fin = o_fin / l_fin[:ncol, :]
    for hh in range(MOBA_HEADS):
        o_ref[0, :, hh * hd:(hh + 1) * hd] = o_fin[hh * t:(hh + 1) * t, hh * hd:(hh + 1) * hd].astype(BF16)


def _moba_sample(q, kn, vn, cache_k, cache_v, page_table, layer):
    b, t, w = q.shape
    n_pages = page_table.shape[1]
    _, _, heads, page, hd = cache_k.shape
    cache_k, cache_v = jnp.swapaxes(cache_k, 3, 4), jnp.swapaxes(cache_v, 3, 4)
    ppb = MOBA_BLOCK // page
    assert MOBA_BLOCK % page == 0 and n_pages % ppb == 0 and heads == MOBA_HEADS and hd == MOBA_HEAD_DIM
    n_full = n_pages // ppb
    assert n_full % SAMPLE_GROUP == 0 and MOBA_TOPK <= n_full <= V7X_LANES and heads * t <= V7X_LANES
    tok = pl.BlockSpec((1, t, w), lambda i, pt: (i, 0, 0))
    any_spec = pl.BlockSpec(memory_space=pl.ANY)
    pages = SAMPLE_GROUP * ppb
    grid_spec = pltpu.PrefetchScalarGridSpec(
        num_scalar_prefetch=1,
        grid=(b,),
        in_specs=[tok, tok, tok, any_spec, any_spec],
        out_specs=tok,
        scratch_shapes=[pltpu.VMEM((2, pages, heads, hd, page), F32), pltpu.VMEM((2, pages, heads, hd, page), F32),
                        pltpu.SemaphoreType.DMA((2,)), pltpu.SemaphoreType.DMA((2,)),
                        pltpu.VMEM((V7X_LANES, V7X_LANES), F32), pltpu.VMEM((V7X_LANES, V7X_LANES), F32),
                        pltpu.VMEM((V7X_LANES, V7X_LANES), F32), pltpu.VMEM((n_full, heads * t, w), F32)],
    )
    return pl.pallas_call(
        functools.partial(_moba_sample_kernel, layer),
        grid_spec=grid_spec,
        out_shape=jax.ShapeDtypeStruct((b, t, w), BF16),
        compiler_params=_cparams(("arbitrary",)),
        name="moba_sample",
    )(page_table, q, kn, vn, cache_k, cache_v)


def _merge_kernel(x_ref, c_ref, om_ref, omem_ref, g1_ref, wg_ref, wco_ref, bco_ref, wmo_ref, wmemo_ref, wout_ref,
                  g2_ref, wr_hi_ref, wr_lo_ref, br_ref, x1_out, h2_out, e_out, gate_out):
    tm, d = x_ref.shape
    x = x_ref[...]
    h = _rms(x, g1_ref[...]).astype(BF16)
    gates = jax.nn.sigmoid(jnp.dot(h, wg_ref[...], preferred_element_type=F32))
    u_conv = jnp.dot(c_ref[...], wco_ref[...], preferred_element_type=F32) + bco_ref[...]
    u_moba = jnp.dot(om_ref[...], wmo_ref[...], preferred_element_type=F32)
    u_mem = jnp.dot(omem_ref[...], wmemo_ref[...], preferred_element_type=F32)
    merged = gates[:, :d] * u_conv + gates[:, d:2 * d] * u_moba + gates[:, 2 * d:] * u_mem
    x1 = x + jnp.dot(merged.astype(BF16), wout_ref[...], preferred_element_type=F32)
    x1_out[...] = x1
    h2 = _rms(x1, g2_ref[...])
    for c in range(d // V7X_LANES):
        h2_out[pl.ds(c, tm, stride=d // V7X_LANES), :] = h2[:, c * V7X_LANES:(c + 1) * V7X_LANES]

    h2_hi = h2.astype(BF16)
    h2_lo = (h2 - h2_hi.astype(F32)).astype(BF16)
    nt = (((1,), (1,)), ((), ()))
    logits = (lax.dot_general(wr_hi_ref[...], h2_hi, nt, preferred_element_type=F32)
              + lax.dot_general(wr_hi_ref[...], h2_lo, nt, preferred_element_type=F32)
              + lax.dot_general(wr_lo_ref[...], h2_hi, nt, preferred_element_type=F32)) + br_ref[...]
    e_id = lax.broadcasted_iota(jnp.int32, logits.shape, 0)
    tops, ids = [], []
    for _ in range(TOP_K):
        mx = jnp.max(logits, axis=0, keepdims=True)
        first = jnp.min(jnp.where(logits == mx, e_id, N_EXPERTS), axis=0, keepdims=True)
        tops.append(mx)
        ids.append(first)
        logits = jnp.where(e_id == first, -jnp.inf, logits)
    ex = [jnp.exp(v - tops[0]) for v in tops]
    denom = ex[0] + ex[1] + ex[2] + ex[3]
    e_out[...] = jnp.concatenate(ids, axis=0)
    gate_rows = jnp.concatenate([v / denom for v in ex] + [jnp.zeros((V7X_LANES - TOP_K, tm), F32)], axis=0)
    gate_out[...] = gate_rows.T


def _merge(x2d, c, om, omem, tm, consts):
    n, d = x2d.shape
    assert n % tm == 0
    row = lambda w: pl.BlockSpec((tm, w), lambda i: (i, 0))
    n_chunk = d // V7X_LANES
    return pl.pallas_call(
        _merge_kernel,
        grid=(n // tm,),
        in_specs=[row(d), row(c.shape[1]), row(om.shape[1]), row(omem.shape[1])] + [_full(a.shape) for a in consts],
        out_specs=[row(d), pl.BlockSpec((tm * n_chunk, V7X_LANES), lambda i: (i, 0)),
                   pl.BlockSpec((TOP_K, tm), lambda i: (0, i)), row(V7X_LANES)],
        out_shape=[jax.ShapeDtypeStruct((n, d), F32), jax.ShapeDtypeStruct((n * n_chunk, V7X_LANES), F32),
                   jax.ShapeDtypeStruct((TOP_K, n), jnp.int32), jax.ShapeDtypeStruct((n, V7X_LANES), F32)],
        compiler_params=_cparams(("parallel",)),
        name="merge",
    )(x2d, c, om, omem, *consts)


META_ROWS = V7X_SUBLANES
TOK_ROW0 = 0
DST_ROW0 = MOE_BLOCK // V7X_LANES


def _moe_kernel(be_ref, meta_hbm, h_hbm, wgu_ref, bgu_ref, wd_ref, bd_ref, o_hbm,
                meta_smem, msem, xbuf, xsem, ybuf, ysem):
    i = pl.program_id(0)
    n = pl.num_programs(0)
    rows = MOE_BLOCK
    n_chunk = xbuf.shape[1] // rows

    def meta_copy(blk):
        return pltpu.make_async_copy(meta_hbm.at[pl.ds(pl.multiple_of(blk * META_ROWS, META_ROWS), META_ROWS)],
                                     meta_smem.at[blk % 3], msem.at[blk % 3])

    def gather_start(blk, slot):
        ms = blk % 3
        for r in range(rows):
            src = pl.multiple_of(meta_smem[ms, TOK_ROW0 + r // V7X_LANES, r % V7X_LANES], n_chunk)
            pltpu.make_async_copy(h_hbm.at[pl.ds(src, n_chunk)], xbuf.at[slot, pl.ds(r * n_chunk, n_chunk)],
                                  xsem.at[slot]).start(priority=r % 2)

    def scatter_start(blk, slot, r0, r1):
        ms = blk % 3
        for r in range(r0, r1):
            dst = pl.multiple_of(meta_smem[ms, DST_ROW0 + r // V7X_LANES, r % V7X_LANES], n_chunk)
            pltpu.make_async_copy(ybuf.at[slot, pl.ds(r * n_chunk, n_chunk)], o_hbm.at[pl.ds(dst, n_chunk)],
                                  ysem.at[slot]).start(priority=r % 2)

    def gather_wait(slot):
        pltpu.make_async_copy(h_hbm.at[pl.ds(0, rows * n_chunk)], xbuf.at[slot], xsem.at[slot]).wait()

    def scatter_wait(slot):
        pltpu.make_async_copy(ybuf.at[slot], o_hbm.at[pl.ds(0, rows * n_chunk)], ysem.at[slot]).wait()

    slot = i % 2

    @pl.when(i == 0)
    def _():
        meta_copy(0).start()
        meta_copy(0).wait()
        gather_start(0, 0)

        @pl.when(n > 1)
        def _():
            meta_copy(1).start()

    @pl.when(i + 2 < n)
    def _():
        meta_copy(i + 2).start()

    @pl.when(i + 1 < n)
    def _():
        meta_copy(i + 1).wait()

    @pl.when(i >= 2)
    def _():
        scatter_wait(slot)

    gather_start(jnp.minimum(i + 1, n - 1), 1 - slot)
    gather_wait(slot)
    x = jnp.concatenate([xbuf[slot, pl.ds(c, rows, stride=n_chunk), :] for c in range(n_chunk)],
                        axis=-1).astype(BF16)
    gu = jnp.dot(x, wgu_ref[0], preferred_element_type=F32) + bgu_ref[0]
    dff = gu.shape[-1] // 2
    g = jnp.minimum(gu[:, :dff], SWIGLU_LIMIT)
    u = jnp.clip(gu[:, dff:], -SWIGLU_LIMIT, SWIGLU_LIMIT)
    act = (g * jax.nn.sigmoid(SWIGLU_ALPHA * g) * (u + 1.0)).astype(BF16)
    y = jnp.dot(act, wd_ref[0], preferred_element_type=F32) + bd_ref[0]
    for c in range(n_chunk):
        ybuf[slot, pl.ds(c, rows, stride=n_chunk), :] = y[:, c * V7X_LANES:(c + 1) * V7X_LANES]
    scatter_start(i, slot, 0, rows)

    @pl.when(i == n - 1)
    def _():
        gather_wait(1 - slot)
        scatter_wait(slot)

        @pl.when(n > 1)
        def _():
            scatter_wait(1 - slot)


def _moe(block_e, meta, h_rows, n_out_rows, wgu, bgu, wd, bd):
    n_blocks = block_e.shape[0]
    d = wgu.shape[1]
    lanes = h_rows.shape[1]
    n_chunk = d // lanes
    any_spec = pl.BlockSpec(memory_space=pl.ANY)
    grid_spec = pltpu.PrefetchScalarGridSpec(
        num_scalar_prefetch=1,
        grid=(n_blocks,),
        in_specs=[any_spec, any_spec,
                  pl.BlockSpec((1, d, wgu.shape[2]), lambda i, be: (be[i], 0, 0)),
                  pl.BlockSpec((1, 1, bgu.shape[2]), lambda i, be: (be[i], 0, 0)),
                  pl.BlockSpec((1, wd.shape[1], d), lambda i, be: (be[i], 0, 0)),
                  pl.BlockSpec((1, 1, d), lambda i, be: (be[i], 0, 0))],
        out_specs=any_spec,
        scratch_shapes=[pltpu.SMEM((3, META_ROWS, V7X_LANES), jnp.int32), pltpu.SemaphoreType.DMA((3,)),
                        pltpu.VMEM((2, MOE_BLOCK * n_chunk, lanes), F32), pltpu.SemaphoreType.DMA((2,)),
                        pltpu.VMEM((2, MOE_BLOCK * n_chunk, lanes), F32), pltpu.SemaphoreType.DMA((2,))],
    )
    return pl.pallas_call(
        _moe_kernel,
        grid_spec=grid_spec,
        out_shape=jax.ShapeDtypeStruct((n_out_rows * n_chunk, lanes), F32),
        compiler_params=_cparams(("arbitrary",)),
        name="moe",
    )(block_e, meta * n_chunk, h_rows, wgu, bgu, wd, bd)


def _route(top_e, n_tok):
    nk = n_tok * TOP_K
    flat_e = top_e.reshape(nk)
    order = jnp.argsort(flat_e).astype(jnp.int32)
    e_sorted = flat_e[order]
    counts = jnp.bincount(flat_e, length=N_EXPERTS).astype(jnp.int32)
    padded = (counts + MOE_BLOCK - 1) // MOE_BLOCK * MOE_BLOCK
    pad_end = jnp.cumsum(padded)
    pad_start = pad_end - padded
    grp_start = jnp.cumsum(counts) - counts
    dest = pad_start[e_sorted] + jnp.arange(nk, dtype=jnp.int32) - grp_start[e_sorted]
    n_blocks = (nk + N_EXPERTS * (MOE_BLOCK - 1) + MOE_BLOCK - 1) // MOE_BLOCK
    n_rows = n_blocks * MOE_BLOCK
    row_asg = jnp.full((n_rows,), -1, jnp.int32).at[dest].set(order, unique_indices=True)
    is_real = row_asg >= 0
    row_tok = jnp.where(is_real, row_asg % n_tok, 0)
    spill = nk + jnp.cumsum(jnp.logical_not(is_real).astype(jnp.int32)) - 1
    row_dst = jnp.where(is_real, row_asg, spill)
    blk_start = jnp.arange(n_blocks, dtype=jnp.int32) * MOE_BLOCK
    block_e = jnp.minimum(jnp.sum((pad_end[None, :] <= blk_start[:, None]).astype(jnp.int32), axis=1), N_EXPERTS - 1)
    per = MOE_BLOCK // V7X_LANES
    meta = jnp.concatenate([row_tok.reshape(n_blocks, per, V7X_LANES), row_dst.reshape(n_blocks, per, V7X_LANES),
                            jnp.zeros((n_blocks, META_ROWS - 2 * per, V7X_LANES), jnp.int32)], axis=1)
    return block_e, meta.reshape(n_blocks * META_ROWS, V7X_LANES), n_rows


def _combine_kernel(x1_ref, gate_ref, o0_ref, o1_ref, o2_ref, o3_ref, y_ref):
    gate = gate_ref[...]
    tm, d = x1_ref.shape
    n_chunk = d // V7X_LANES
    cols = [gate[:, k:k + 1] for k in range(TOP_K)]
    for c in range(n_chunk):
        sl = slice(c * V7X_LANES, (c + 1) * V7X_LANES)
        acc = x1_ref[:, sl]
        for k, o_ref in enumerate((o0_ref, o1_ref, o2_ref, o3_ref)):
            acc = acc + cols[k] * o_ref[pl.ds(c, tm, stride=n_chunk), :]
        y_ref[:, sl] = acc


def _combine(x1, gate_col, o_rows, tm):
    n, d = x1.shape
    lanes = o_rows.shape[1]
    n_chunk = d // lanes
    assert n % tm == 0
    per = n // tm
    o_spec = lambda k: pl.BlockSpec((tm * n_chunk, lanes), lambda i, k=k: (k * per + i, 0))
    return pl.pallas_call(
        _combine_kernel,
        grid=(per,),
        in_specs=[pl.BlockSpec((tm, d), lambda i: (i, 0)), pl.BlockSpec((tm, V7X_LANES), lambda i: (i, 0))]
                 + [o_spec(k) for k in range(TOP_K)],
        out_specs=pl.BlockSpec((tm, d), lambda i: (i, 0)),
        out_shape=jax.ShapeDtypeStruct((n, d), F32),
        compiler_params=_cparams(("parallel",)),
        name="combine",
    )(x1, gate_col, o_rows, o_rows, o_rows, o_rows)


def _prep_weights(p):
    cw = p["w_dw"].shape[-1]
    o_gate = 2 * cw + 3 * MOBA_WIDTH + MEM_WIDTH
    row = lambda a: a.reshape(1, -1).astype(F32)
    tile = lambda g, n: jnp.tile(g.astype(F32), n).reshape(1, -1)
    prep = {}
    prep["proj"] = (
        row(p["g_norm1"]), p["w_in"][:, :o_gate].astype(BF16),
        _segment_ones(MOBA_WIDTH, MOBA_HEAD_DIM), _segment_ones(MEM_WIDTH, MEM_HEAD_DIM),
        tile(p["g_q_moba"], MOBA_HEADS), tile(p["g_k_moba"], MOBA_HEADS), tile(p["g_q_mem"], MEM_HEADS),
        p["w_dw"].astype(F32), row(p["b_dw"]), row(p["g_conv_ln"]), row(p["b_conv_ln"]),
    )
    prep["mem_kv"] = (row(p["g_mem_norm"]), p["w_mem_kv"].astype(BF16), _segment_ones(MEM_WIDTH, MEM_HEAD_DIM),
                      tile(p["g_k_mem"], MEM_HEADS))
    wr_t = p["w_router"].astype(F32).T
    wr_hi = wr_t.astype(BF16)
    wr_lo = (wr_t - wr_hi.astype(F32)).astype(BF16)
    prep["merge"] = (
        row(p["g_norm1"]), p["w_in"][:, o_gate:].astype(BF16), p["w_conv_out"].astype(BF16), row(p["b_conv_out"]),
        p["w_moba_o"].astype(BF16), p["w_mem_o"].astype(BF16), p["w_out"].astype(BF16), row(p["g_norm2"]),
        wr_hi, wr_lo, p["b_router"].astype(F32).reshape(-1, 1),
    )
    prep["moe"] = (p["w_gu"].astype(BF16), p["b_gu"].astype(F32)[:, None, :], p["w_down"].astype(BF16),
                   p["b_down"].astype(F32)[:, None, :])
    return prep


def _ffn(x2d, c, om, omem, tm, prep):
    n = x2d.shape[0]
    x1, h_rows, top_e, gate_col = _merge(x2d, c, om, omem, tm, prep["merge"])
    block_e, meta, n_rows = _route(top_e, n)
    o_rows = _moe(block_e, meta, h_rows, n_rows, *prep["moe"])
    return _combine(x1, gate_col, o_rows, tm)


_PARAM_NAMES = ("g_norm1", "w_in", "w_dw", "b_dw", "g_conv_ln", "b_conv_ln", "w_conv_out", "b_conv_out", "g_q_moba",
                "g_k_moba", "w_moba_o", "g_mem_norm", "w_mem_kv", "g_q_mem", "g_k_mem", "w_mem_o", "w_out", "g_norm2",
                "w_router", "b_router", "w_gu", "b_gu", "w_down", "b_down")

PROMPT_ROWS = 512
SAMPLE_BATCH_TILE = 64
SAMPLE_MEM_TILE = 8


def kernel(x_prompt, x_sample, mem_prompt, cache_moba_k, cache_moba_v, cache_mem_k, cache_mem_v, state_conv, page_table, g_norm1, w_in, w_dw, b_dw, g_conv_ln, b_conv_ln, w_conv_out, b_conv_out, g_q_moba, g_k_moba, w_moba_o, g_mem_norm, w_mem_kv, g_q_mem, g_k_mem, w_mem_o, w_out, g_norm2, w_router, b_router, w_gu, b_gu, w_down, b_down):
    params = dict(zip(_PARAM_NAMES, (g_norm1, w_in, w_dw, b_dw, g_conv_ln, b_conv_ln, w_conv_out, b_conv_out,
                                     g_q_moba, g_k_moba, w_moba_o, g_mem_norm, w_mem_kv, g_q_mem, g_k_mem, w_mem_o,
                                     w_out, g_norm2, w_router, b_router, w_gu, b_gu, w_down, b_down)))
    depth = g_norm1.shape[0]
    bp, lp, d = x_prompt.shape
    bs, ls, _ = x_sample.shape
    n_mem = mem_prompt.shape[1]
    y_p, y_s = x_prompt, x_sample
    outs = [[] for _ in range(8)]
    for layer in range(depth):
        prep = _prep_weights({k: v[layer] for k, v in params.items()})
        k, v, q, kb, vb, qm, c, tail = _proj(y_p, None, 1, PROMPT_ROWS, *prep["proj"])
        mk, mv = _mem_kv(mem_prompt.reshape(bp * n_mem, d), PROMPT_ROWS, *prep["mem_kv"])
        o_moba = _moba_prompt(q, kb, vb)
        o_mem = _mem_attend(qm, mk.reshape(1, bp, n_mem, MEM_WIDTH), mv.reshape(1, bp, n_mem, MEM_WIDTH), 0, 1,
                            PROMPT_ROWS)
        flat = lambda a: a.reshape(bp * lp, a.shape[-1])
        y_p = _ffn(flat(y_p), flat(c), flat(o_moba), flat(o_mem), PROMPT_ROWS, prep).reshape(bp, lp, d)
        mem_shape = (bp, n_mem, MEM_HEADS, MEM_HEAD_DIM)
        for lst, val in zip(outs[:5], (k, v, mk.reshape(mem_shape), mv.reshape(mem_shape), tail)):
            lst.append(val)
        k, v, q, kb, vb, qm, c, tail = _proj(y_s, state_conv[layer], SAMPLE_BATCH_TILE, ls, *prep["proj"])
        o_moba = _moba_sample(q, kb, vb, cache_moba_k, cache_moba_v, page_table, layer)
        o_mem = _mem_attend(qm, cache_mem_k.reshape(depth, bs, n_mem, MEM_WIDTH),
                            cache_mem_v.reshape(depth, bs, n_mem, MEM_WIDTH), layer, SAMPLE_MEM_TILE, ls)
        flat = lambda a: a.reshape(bs * ls, a.shape[-1])
        y_s = _ffn(flat(y_s), flat(c), flat(o_moba), flat(o_mem), PROMPT_ROWS, prep).reshape(bs, ls, d)
        for lst, val in zip(outs[5:], (k, v, tail)):
            lst.append(val)
    kp, vp, mkp, mvp, cp, ks, vs, cs = (jnp.stack(o) for o in outs)
    return (y_p, y_s, kp, vp, mkp, mvp, cp, ks, vs, cs)
```

```python
import functools

import jax
import jax.numpy as jnp
from jax import lax
from jax.experimental import pallas as pl
from jax.experimental.pallas import tpu as pltpu

F32 = jnp.float32
BF16 = jnp.bfloat16

EPS = 1e-6
MOBA_HEADS = 8
MOBA_HEAD_DIM = 64
MOBA_WIDTH = MOBA_HEADS * MOBA_HEAD_DIM
MOBA_BLOCK = 256
MOBA_TOPK = 3
MEM_HEADS = 4
MEM_HEAD_DIM = 128
MEM_WIDTH = MEM_HEADS * MEM_HEAD_DIM
CONV_KERNEL = 31
CONV_HIST = CONV_KERNEL - 1
N_EXPERTS = 32
TOP_K = 4
SWIGLU_LIMIT = 7.0
SWIGLU_ALPHA = 1.702
MOE_BLOCK = 256

V7X_LANES = 128
V7X_SUBLANES = 8
HIST_PAD = 32
VMEM_LIMIT = 56 * 1024 * 1024


def _cparams(sem):
    return pltpu.CompilerParams(dimension_semantics=sem, vmem_limit_bytes=VMEM_LIMIT)


def _full(shape):
    n = len(shape)
    return pl.BlockSpec(shape, lambda *_: (0,) * n)


def _segment_ones(width, seg):
    r = jnp.arange(width) // seg
    return (r[:, None] == r[None, :]).astype(BF16)


def _rms(x, g):
    return x * lax.rsqrt(jnp.mean(x * x, axis=-1, keepdims=True) + EPS) * g


def _seg_rms(z, seg_ones, seg, g):
    sq = (z * z).astype(BF16)
    parts = []
    for c in range(0, z.shape[-1], 2 * V7X_LANES):
        parts.append(jnp.dot(sq[:, c:c + 2 * V7X_LANES], seg_ones[c:c + 2 * V7X_LANES, c:c + 2 * V7X_LANES],
                             preferred_element_type=F32))
    ms = jnp.concatenate(parts, axis=-1) * (1.0 / seg)
    return z * lax.rsqrt(ms + EPS) * g


def _proj_kernel(has_hist, carry, x_ref, *refs):
    if has_hist:
        hist_ref, refs = refs[0], refs[1:]
    (g1_ref, w1_ref, seg64_ref, seg128_ref, gq_ref, gk_ref, gqm_ref, wdw_ref, bdw_ref, gln_ref, bln_ref,
     k_out, v_out, q_out, kb_out, vb_out, qm_out, c_out, tail_out, xp_scr) = refs
    bt, tl, d = x_ref.shape
    rows = bt * tl
    cw = c_out.shape[-1]
    t = pl.program_id(1)

    x = x_ref[...].reshape(rows, d)
    h = _rms(x, g1_ref[...]).astype(BF16)
    z = jnp.dot(h, w1_ref[...], preferred_element_type=F32)
    o_q = 2 * cw
    o_k = o_q + MOBA_WIDTH
    o_v = o_k + MOBA_WIDTH
    o_qm = o_v + MOBA_WIDTH
    u = z[:, :cw] * jax.nn.sigmoid(z[:, cw:o_q])
    seg64 = seg64_ref[...]
    qn = _seg_rms(z[:, o_q:o_k], seg64, MOBA_HEAD_DIM, gq_ref[...])
    kn = _seg_rms(z[:, o_k:o_v], seg64, MOBA_HEAD_DIM, gk_ref[...])
    vv = z[:, o_v:o_qm]
    qmn = _seg_rms(z[:, o_qm:o_qm + MEM_WIDTH], seg128_ref[...], MEM_HEAD_DIM, gqm_ref[...])

    q_out[...] = qn.astype(BF16).reshape(bt, tl, MOBA_WIDTH)
    kb_out[...] = kn.astype(BF16).reshape(bt, tl, MOBA_WIDTH)
    vb_out[...] = vv.astype(BF16).reshape(bt, tl, MOBA_WIDTH)
    qm_out[...] = qmn.astype(BF16).reshape(bt, tl, MEM_WIDTH)
    kn3 = kn.reshape(bt, tl, MOBA_WIDTH)
    vv3 = vv.reshape(bt, tl, MOBA_WIDTH)
    for hh in range(MOBA_HEADS):
        sl = slice(hh * MOBA_HEAD_DIM, (hh + 1) * MOBA_HEAD_DIM)
        k_out[:, hh, :, :] = kn3[:, :, sl]
        v_out[:, hh, :, :] = vv3[:, :, sl]

    lo = HIST_PAD - CONV_HIST

    @pl.when(t == 0)
    def _():
        xp_scr[:, 0:HIST_PAD, :] = jnp.zeros((bt, HIST_PAD, cw), F32)
        if has_hist:
            xp_scr[:, lo:HIST_PAD, :] = hist_ref[...]

    xp_scr[:, HIST_PAD:HIST_PAD + tl, :] = u.reshape(bt, tl, cw)
    rc = min(tl, 64)
    bdw = bdw_ref[...]
    gln = gln_ref[...]
    bln = bln_ref[...]
    for r0 in range(0, tl, rc):
        window = xp_scr[:, r0:r0 + rc + HIST_PAD, :]
        acc = jnp.zeros((bt, rc, cw), F32) + bdw
        for s in range(V7X_SUBLANES):
            ext = rc if s == 0 else rc + V7X_SUBLANES
            part = None
            for q in range(HIST_PAD // V7X_SUBLANES + 1):
                j = V7X_SUBLANES * q + s - lo
                if 0 <= j < CONV_KERNEL:
                    term = wdw_ref[j:j + 1, :] * window[:, V7X_SUBLANES * q:V7X_SUBLANES * q + ext, :]
                    part = term if part is None else part + term
            acc = acc + (part if s == 0 else part[:, s:s + rc, :])
        mu = jnp.mean(acc, axis=-1, keepdims=True)
        dev = acc - mu
        var = jnp.mean(dev * dev, axis=-1, keepdims=True)
        y = dev * lax.rsqrt(var + EPS) * gln + bln
        c_out[:, r0:r0 + rc, :] = (y * jax.nn.sigmoid(y)).astype(BF16)
    tail_out[...] = xp_scr[:, tl + lo:tl + HIST_PAD, :]
    if carry:
        xp_scr[:, 0:HIST_PAD, :] = xp_scr[:, tl:tl + HIST_PAD, :]


def _proj(x, hist, bt, tl, g1, w1, seg64, seg128, gq, gk, gqm, wdw, bdw, gln, bln):
    b, l, d = x.shape
    cw = wdw.shape[-1]
    n_t = l // tl
    has_hist = hist is not None
    carry = n_t > 1
    assert b % bt == 0 and l % tl == 0 and (not carry or (tl >= HIST_PAD and bt == 1))
    tok = lambda w: pl.BlockSpec((bt, tl, w), lambda i, j: (i, j, 0))
    head = pl.BlockSpec((bt, MOBA_HEADS, tl, MOBA_HEAD_DIM), lambda i, j: (i, 0, j, 0))
    in_specs = [tok(d)]
    args = [x]
    if has_hist:
        in_specs.append(pl.BlockSpec((bt, CONV_HIST, cw), lambda i, j: (i, 0, 0)))
        args.append(hist)
    consts = [g1, w1, seg64, seg128, gq, gk, gqm, wdw, bdw, gln, bln]
    in_specs += [_full(c.shape) for c in consts]
    out_shape = [
        jax.ShapeDtypeStruct((b, MOBA_HEADS, l, MOBA_HEAD_DIM), F32),
        jax.ShapeDtypeStruct((b, MOBA_HEADS, l, MOBA_HEAD_DIM), F32),
        jax.ShapeDtypeStruct((b, l, MOBA_WIDTH), BF16),
        jax.ShapeDtypeStruct((b, l, MOBA_WIDTH), BF16),
        jax.ShapeDtypeStruct((b, l, MOBA_WIDTH), BF16),
        jax.ShapeDtypeStruct((b, l, MEM_WIDTH), BF16),
        jax.ShapeDtypeStruct((b, l, cw), BF16),
        jax.ShapeDtypeStruct((b, CONV_HIST, cw), F32),
    ]
    out_specs = [head, head, tok(MOBA_WIDTH), tok(MOBA_WIDTH), tok(MOBA_WIDTH), tok(MEM_WIDTH), tok(cw),
                 pl.BlockSpec((bt, CONV_HIST, cw), lambda i, j: (i, 0, 0))]
    return pl.pallas_call(
        functools.partial(_proj_kernel, has_hist, carry),
        grid=(b // bt, n_t),
        in_specs=in_specs,
        out_specs=out_specs,
        out_shape=out_shape,
        scratch_shapes=[pltpu.VMEM((bt, HIST_PAD + tl, cw), F32)],
        compiler_params=_cparams(("parallel", "arbitrary")),
        name="proj",
    )(*args, *consts)


def _memkv_kernel(x_ref, g_ref, w_ref, seg128_ref, gk_ref, k_out, v_out):
    h = _rms(x_ref[...], g_ref[...]).astype(BF16)
    kv = jnp.dot(h, w_ref[...], preferred_element_type=F32)
    k_out[...] = _seg_rms(kv[:, :MEM_WIDTH], seg128_ref[...], MEM_HEAD_DIM, gk_ref[...])
    v_out[...] = kv[:, MEM_WIDTH:]


def _mem_kv(mem2d, tm, g, w, seg128, gk):
    n, d = mem2d.shape
    assert n % tm == 0
    spec = pl.BlockSpec((tm, MEM_WIDTH), lambda i: (i, 0))
    return pl.pallas_call(
        _memkv_kernel,
        grid=(n // tm,),
        in_specs=[pl.BlockSpec((tm, d), lambda i: (i, 0))] + [_full(c.shape) for c in (g, w, seg128, gk)],
        out_specs=[spec, spec],
        out_shape=[jax.ShapeDtypeStruct((n, MEM_WIDTH), F32)] * 2,
        compiler_params=_cparams(("parallel",)),
        name="mem_kv",
    )(mem2d, g, w, seg128, gk)


def _mematt_kernel(q_ref, k_ref, v_ref, o_ref):
    scale = MEM_HEAD_DIM ** -0.5
    for hh in range(MEM_HEADS):
        sl = slice(hh * MEM_HEAD_DIM, (hh + 1) * MEM_HEAD_DIM)
        q = q_ref[:, :, sl]
        k = k_ref[0, :, :, sl].astype(BF16)
        v = v_ref[0, :, :, sl].astype(BF16)
        s = jnp.einsum("bqd,bkd->bqk", q, k, preferred_element_type=F32) * scale
        m = jnp.max(s, axis=-1, keepdims=True)
        p = jnp.exp(s - m)
        l = jnp.sum(p, axis=-1, keepdims=True)
        o = jnp.einsum("bqk,bkd->bqd", p.astype(BF16), v, preferred_element_type=F32)
        o_ref[:, :, sl] = (o / l).astype(BF16)


def _mem_attend(qm, mk, mv, layer, bt, tq):
    b, l, w = qm.shape
    m = mk.shape[2]
    assert b % bt == 0 and l % tq == 0
    return pl.pallas_call(
        _mematt_kernel,
        grid=(b // bt, l // tq),
        in_specs=[pl.BlockSpec((bt, tq, w), lambda i, j: (i, j, 0)),
                  pl.BlockSpec((1, bt, m, w), lambda i, j: (layer, i, 0, 0)),
                  pl.BlockSpec((1, bt, m, w), lambda i, j: (layer, i, 0, 0))],
        out_specs=pl.BlockSpec((bt, tq, w), lambda i, j: (i, j, 0)),
        out_shape=jax.ShapeDtypeStruct((b, l, w), BF16),
        compiler_params=_cparams(("parallel", "arbitrary")),
        name="mem_attend",
    )(qm, mk, mv)


def _moba_prompt_kernel(q_ref, k_ref, v_ref, o_ref, vt_scr, km_scr, sel_scr, acc_scr, m_scr, l_scr, wq_scr):
    blk = MOBA_BLOCK
    l = k_ref.shape[1]
    nb = l // blk
    qb = pl.program_id(1)
    hd = MOBA_HEAD_DIM

    @pl.when(qb == 0)
    def _():
        lane_head = lax.broadcasted_iota(jnp.int32, (MOBA_HEADS, MOBA_WIDTH), 1) // hd
        row_head = lax.broadcasted_iota(jnp.int32, (MOBA_HEADS, MOBA_WIDTH), 0)
        head_mask = (lane_head == row_head).astype(F32)
        for n in range(nb):
            vt_scr[n] = v_ref[0, n * blk:(n + 1) * blk, :].T
            kmean = jnp.mean(k_ref[0, n * blk:(n + 1) * blk, :].astype(F32), axis=0, keepdims=True)
            km_scr[n * MOBA_HEADS:(n + 1) * MOBA_HEADS, :] = (kmean * head_mask).astype(BF16)

    qt = q_ref[0].T
    gate = jnp.dot(km_scr[...], qt, preferred_element_type=F32).reshape(nb, MOBA_HEADS, blk)
    for n in range(nb):
        rank = jnp.zeros((MOBA_HEADS, blk), jnp.int32)
        for n2 in range(nb):
            if n2 == n:
                continue
            beats = (gate[n2] >= gate[n]) if n2 < n else (gate[n2] > gate[n])
            rank = rank + jnp.where(jnp.logical_and(beats, n2 < qb), 1, 0)
        sel_scr[n] = jnp.where(jnp.logical_and(rank < MOBA_TOPK, n < qb), 1.0, 0.0)

    pairs = MOBA_HEADS // 2
    row_in_pair = lax.broadcasted_iota(jnp.int32, (2 * hd, blk), 0) // hd
    for pr in range(pairs):
        qp = qt[pr * 2 * hd:(pr + 1) * 2 * hd, :] * jnp.asarray(hd ** -0.5, BF16)
        zero = jnp.zeros_like(qp)
        wq_scr[pr] = jnp.concatenate([jnp.where(row_in_pair == 0, qp, zero), jnp.where(row_in_pair == 1, qp, zero)],
                                     axis=1)

    def attend(kblk, n, keep, first):
        s_pairs = [jnp.dot(kblk[:, pr * 2 * hd:(pr + 1) * 2 * hd], wq_scr[pr], preferred_element_type=F32)
                   for pr in range(pairs)]
        m_old = None if first else m_scr[...]
        l_old = None if first else l_scr[...]
        m_rows, l_rows, alphas, probs = [], [], [], []
        for hh in range(MOBA_HEADS):
            s = jnp.where(keep(hh), s_pairs[hh // 2][:, (hh % 2) * blk:(hh % 2 + 1) * blk], -jnp.inf)
            m_new = jnp.max(s, axis=0, keepdims=True)
            if not first:
                m_new = jnp.maximum(m_old[hh:hh + 1, :], m_new)
                alphas.append(jnp.exp(m_old[hh:hh + 1, :] - m_new))
            p = jnp.exp(s - m_new)
            p_sum = jnp.sum(p, axis=0, keepdims=True)
            l_rows.append(p_sum if first else alphas[hh] * l_old[hh:hh + 1, :] + p_sum)
            m_rows.append(m_new)
            probs.append(p.astype(BF16))
        m_scr[...] = jnp.concatenate(m_rows, axis=0)
        l_scr[...] = jnp.concatenate(l_rows, axis=0)
        pvs = [jnp.dot(vt_scr[n, hh * hd:(hh + 1) * hd, :], probs[hh], preferred_element_type=F32)
               for hh in range(MOBA_HEADS)]
        for hh in range(MOBA_HEADS):
            rows = slice(hh * hd, (hh + 1) * hd)
            acc_scr[rows, :] = pvs[hh] if first else alphas[hh] * acc_scr[rows, :] + pvs[hh]

    causal = lax.broadcasted_iota(jnp.int32, (blk, blk), 0) <= lax.broadcasted_iota(jnp.int32, (blk, blk), 1)
    attend(k_ref[0, pl.ds(pl.multiple_of(qb * blk, blk), blk), :], qb, lambda hh: causal, True)

    def past_block(n, carry):
        sel = sel_scr[n]
        attend(k_ref[0, pl.ds(pl.multiple_of(n * blk, blk), blk), :], n, lambda hh: sel[hh:hh + 1, :] > 0.0, False)
        return carry

    lax.fori_loop(0, qb, past_block, 0)

    for hh in range(MOBA_HEADS):
        acc_scr[hh * hd:(hh + 1) * hd, :] = acc_scr[hh * hd:(hh + 1) * hd, :] / l_scr[hh:hh + 1, :]
    o_ref[0] = acc_scr[...].T.astype(BF16)


def _moba_prompt(q, kb, vb):
    b, l, w = q.shape
    blk = MOBA_BLOCK
    assert l % blk == 0
    nb = l // blk
    full = pl.BlockSpec((1, l, w), lambda i, j: (i, 0, 0))
    tile = pl.BlockSpec((1, blk, w), lambda i, j: (i, j, 0))
    return pl.pallas_call(
        _moba_prompt_kernel,
        grid=(b, nb),
        in_specs=[tile, full, full],
        out_specs=tile,
        out_shape=jax.ShapeDtypeStruct((b, l, w), BF16),
        scratch_shapes=[pltpu.VMEM((nb, w, blk), BF16), pltpu.VMEM((nb * MOBA_HEADS, w), BF16),
                        pltpu.VMEM((nb, MOBA_HEADS, blk), F32), pltpu.VMEM((w, blk), F32),
                        pltpu.VMEM((MOBA_HEADS, blk), F32), pltpu.VMEM((MOBA_HEADS, blk), F32),
                        pltpu.VMEM((MOBA_HEADS // 2, 2 * MOBA_HEAD_DIM, 2 * blk), BF16)],
        compiler_params=_cparams(("parallel", "arbitrary")),
        name="moba_prompt",
    )(q, kb, vb)


SAMPLE_GROUP = 4


def _moba_sample_kernel(layer, pt_ref, q_ref, kn_ref, vn_ref, ck_hbm, cv_hbm, o_ref,
                        kbuf, vbuf, ksem, vsem, m_scr, l_scr, g_scr, o_scr):
    b = pl.program_id(0)
    nb = pl.num_programs(0)
    n_pages = pt_ref.shape[1]
    page = kbuf.shape[4]
    ppb = MOBA_BLOCK // page
    n_full = n_pages // ppb
    grp = SAMPLE_GROUP
    n_grp = n_full // grp
    hd = MOBA_HEAD_DIM
    t = q_ref.shape[1]
    ncol = MOBA_HEADS * t
    scale = hd ** -0.5
    nt = (((1,), (1,)), ((), ()))

    def copies(bb, g, par):
        out = []
        for j in range(grp):
            for pg in range(ppb):
                pid = pt_ref[bb, (g * grp + j) * ppb + pg]
                out.append(pltpu.make_async_copy(ck_hbm.at[layer, pid], kbuf.at[par, j * ppb + pg], ksem.at[par]))
                out.append(pltpu.make_async_copy(cv_hbm.at[layer, pid], vbuf.at[par, j * ppb + pg], vsem.at[par]))
        return out

    @pl.when(b == 0)
    def _():
        for c in copies(0, 0, 0):
            c.start()

    q = q_ref[0]
    row_head = lax.broadcasted_iota(jnp.int32, (V7X_LANES, MOBA_WIDTH), 0) // t
    lane_head = lax.broadcasted_iota(jnp.int32, (V7X_LANES, MOBA_WIDTH), 1) // hd
    q_tiled = jnp.concatenate([q.astype(F32)] * (V7X_LANES // t), axis=0)
    qrows = jnp.where(row_head == lane_head, q_tiled, 0.0).astype(BF16)

    def packed(buf, par, j):
        return jnp.concatenate([buf[par, j * ppb + pg].reshape(MOBA_WIDTH, page) for pg in range(ppb)],
                               axis=-1).astype(BF16)

    lane_id = lax.broadcasted_iota(jnp.int32, (V7X_LANES, V7X_LANES), 1)
    g_scr[...] = jnp.full((V7X_LANES, V7X_LANES), -jnp.inf, F32)
    m_scr[...] = jnp.full((V7X_LANES, V7X_LANES), -jnp.inf, F32)
    l_scr[...] = jnp.zeros((V7X_LANES, V7X_LANES), F32)

    def group(g, carry):
        par = (b * n_grp + g) % 2

        @pl.when(g + 1 < n_grp)
        def _():
            for c in copies(b, g + 1, 1 - par):
                c.start()

        @pl.when(jnp.logical_and(g + 1 == n_grp, b + 1 < nb))
        def _():
            for c in copies(b + 1, 0, 1 - par):
                c.start()

        for c in copies(b, g, par):
            c.wait()
        for j in range(grp):
            n = g * grp + j
            s_raw = jnp.dot(qrows, packed(kbuf, par, j), preferred_element_type=F32)
            s = s_raw * scale
            m = jnp.max(s, axis=-1, keepdims=True)
            p = jnp.exp(s - m)
            hit = lane_id == n
            g_scr[...] = jnp.where(hit, jnp.sum(s_raw, axis=-1, keepdims=True) * (1.0 / MOBA_BLOCK), g_scr[...])
            m_scr[...] = jnp.where(hit, m, m_scr[...])
            l_scr[...] = jnp.where(hit, jnp.sum(p, axis=-1, keepdims=True), l_scr[...])
            o_scr[n] = lax.dot_general(p[:ncol, :].astype(BF16), packed(vbuf, par, j), nt, preferred_element_type=F32)
        return carry

    lax.fori_loop(0, n_grp, group, 0)

    gate = g_scr[...]
    sel = jnp.zeros(gate.shape, jnp.bool_)
    for _ in range(min(MOBA_TOPK, n_full)):
        mx = jnp.max(gate, axis=-1, keepdims=True)
        first = jnp.min(jnp.where(gate == mx, lane_id, V7X_LANES), axis=-1, keepdims=True)
        hit = lane_id == first
        sel = jnp.logical_or(sel, hit)
        gate = jnp.where(hit, -jnp.inf, gate)

    s_new = lax.dot_general(qrows, kn_ref[0], nt, preferred_element_type=F32) * scale
    row_tok = lax.broadcasted_iota(jnp.int32, s_new.shape, 0) % t
    key_tok = lax.broadcasted_iota(jnp.int32, s_new.shape, 1)
    s_new = jnp.where(key_tok <= row_tok, s_new, -jnp.inf)
    m_blk = jnp.where(sel, m_scr[...], -jnp.inf)
    m_fin = jnp.maximum(jnp.max(m_blk, axis=-1, keepdims=True), jnp.max(s_new, axis=-1, keepdims=True))
    w_blk = jnp.where(sel, jnp.exp(m_blk - m_fin), 0.0)
    p_new = jnp.exp(s_new - m_fin)
    l_fin = jnp.sum(w_blk * l_scr[...], axis=-1, keepdims=True) + jnp.sum(p_new, axis=-1, keepdims=True)
    o_fin = jnp.dot(p_new[:ncol, :].astype(BF16), vn_ref[0], preferred_element_type=F32)
    for n in range(n_full):
        o_fin = o_fin + w_blk[:ncol, n:n + 1] * o_scr[n]
    o_fin = o_fin / l_fin[:ncol, :]
    for hh in range(MOBA_HEADS):
        o_ref[0, :, hh * hd:(hh + 1) * hd] = o_fin[hh * t:(hh + 1) * t, hh * hd:(hh + 1) * hd].astype(BF16)


def _moba_sample(q, kn, vn, cache_k, cache_v, page_table, layer):
    b, t, w = q.shape
    n_pages = page_table.shape[1]
    _, _, heads, page, hd = cache_k.shape
    cache_k, cache_v = jnp.swapaxes(cache_k, 3, 4), jnp.swapaxes(cache_v, 3, 4)
    ppb = MOBA_BLOCK // page
    assert MOBA_BLOCK % page == 0 and n_pages % ppb == 0 and heads == MOBA_HEADS and hd == MOBA_HEAD_DIM
    n_full = n_pages // ppb
    assert n_full % SAMPLE_GROUP == 0 and MOBA_TOPK <= n_full <= V7X_LANES and heads * t <= V7X_LANES
    tok = pl.BlockSpec((1, t, w), lambda i, pt: (i, 0, 0))
    any_spec = pl.BlockSpec(memory_space=pl.ANY)
    pages = SAMPLE_GROUP * ppb
    grid_spec = pltpu.PrefetchScalarGridSpec(
        num_scalar_prefetch=1,
        grid=(b,),
        in_specs=[tok, tok, tok, any_spec, any_spec],
        out_specs=tok,
        scratch_shapes=[pltpu.VMEM((2, pages, heads, hd, page), F32), pltpu.VMEM((2, pages, heads, hd, page), F32),
                        pltpu.SemaphoreType.DMA((2,)), pltpu.SemaphoreType.DMA((2,)),
                        pltpu.VMEM((V7X_LANES, V7X_LANES), F32), pltpu.VMEM((V7X_LANES, V7X_LANES), F32),
                        pltpu.VMEM((V7X_LANES, V7X_LANES), F32), pltpu.VMEM((n_full, heads * t, w), F32)],
    )
    return pl.pallas_call(
        functools.partial(_moba_sample_kernel, layer),
        grid_spec=grid_spec,
        out_shape=jax.ShapeDtypeStruct((b, t, w), BF16),
        compiler_params=_cparams(("arbitrary",)),
        name="moba_sample",
    )(page_table, q, kn, vn, cache_k, cache_v)


def _merge_kernel(x_ref, c_ref, om_ref, omem_ref, g1_ref, wg_ref, wco_ref, bco_ref, wmo_ref, wmemo_ref, wout_ref,
                  g2_ref, wr_hi_ref, wr_lo_ref, br_ref, x1_out, h2_out, e_out, gate_out):
    tm, d = x_ref.shape
    x = x_ref[...]
    h = _rms(x, g1_ref[...]).astype(BF16)
    gates = jax.nn.sigmoid(jnp.dot(h, wg_ref[...], preferred_element_type=F32))
    u_conv = jnp.dot(c_ref[...], wco_ref[...], preferred_element_type=F32) + bco_ref[...]
    u_moba = jnp.dot(om_ref[...], wmo_ref[...], preferred_element_type=F32)
    u_mem = jnp.dot(omem_ref[...], wmemo_ref[...], preferred_element_type=F32)
    merged = gates[:, :d] * u_conv + gates[:, d:2 * d] * u_moba + gates[:, 2 * d:] * u_mem
    x1 = x + jnp.dot(merged.astype(BF16), wout_ref[...], preferred_element_type=F32)
    x1_out[...] = x1
    h2 = _rms(x1, g2_ref[...])
    for c in range(d // V7X_LANES):
        h2_out[pl.ds(c, tm, stride=d // V7X_LANES), :] = h2[:, c * V7X_LANES:(c + 1) * V7X_LANES]

    h2_hi = h2.astype(BF16)
    h2_lo = (h2 - h2_hi.astype(F32)).astype(BF16)
    nt = (((1,), (1,)), ((), ()))
    logits = (lax.dot_general(wr_hi_ref[...], h2_hi, nt, preferred_element_type=F32)
              + lax.dot_general(wr_hi_ref[...], h2_lo, nt, preferred_element_type=F32)
              + lax.dot_general(wr_lo_ref[...], h2_hi, nt, preferred_element_type=F32)) + br_ref[...]
    e_id = lax.broadcasted_iota(jnp.int32, logits.shape, 0)
    tops, ids = [], []
    for _ in range(TOP_K):
        mx = jnp.max(logits, axis=0, keepdims=True)
        first = jnp.min(jnp.where(logits == mx, e_id, N_EXPERTS), axis=0, keepdims=True)
        tops.append(mx)
        ids.append(first)
        logits = jnp.where(e_id == first, -jnp.inf, logits)
    ex = [jnp.exp(v - tops[0]) for v in tops]
    denom = ex[0] + ex[1] + ex[2] + ex[3]
    e_out[...] = jnp.concatenate(ids, axis=0)
    gate_rows = jnp.concatenate([v / denom for v in ex] + [jnp.zeros((V7X_LANES - TOP_K, tm), F32)], axis=0)
    gate_out[...] = gate_rows.T


def _merge(x2d, c, om, omem, tm, consts):
    n, d = x2d.shape
    assert n % tm == 0
    row = lambda w: pl.BlockSpec((tm, w), lambda i: (i, 0))
    n_chunk = d // V7X_LANES
    return pl.pallas_call(
        _merge_kernel,
        grid=(n // tm,),
        in_specs=[row(d), row(c.shape[1]), row(om.shape[1]), row(omem.shape[1])] + [_full(a.shape) for a in consts],
        out_specs=[row(d), pl.BlockSpec((tm * n_chunk, V7X_LANES), lambda i: (i, 0)),
                   pl.BlockSpec((TOP_K, tm), lambda i: (0, i)), row(V7X_LANES)],
        out_shape=[jax.ShapeDtypeStruct((n, d), F32), jax.ShapeDtypeStruct((n * n_chunk, V7X_LANES), F32),
                   jax.ShapeDtypeStruct((TOP_K, n), jnp.int32), jax.ShapeDtypeStruct((n, V7X_LANES), F32)],
        compiler_params=_cparams(("parallel",)),
        name="merge",
    )(x2d, c, om, omem, *consts)


META_ROWS = V7X_SUBLANES
TOK_ROW0 = 0
DST_ROW0 = MOE_BLOCK // V7X_LANES


def _moe_kernel(first_spill, be_ref, meta_hbm, h_hbm, wgu_ref, bgu_ref, wd_ref, bd_ref, o_hbm,
                meta_smem, msem, xbuf, xsem, ybuf, ysem, wgu_bf, wd_bf):
    i = pl.program_id(0)
    n = be_ref[pl.num_programs(0)]
    rows = MOE_BLOCK
    n_chunk = xbuf.shape[1] // rows

    def meta_copy(blk):
        return pltpu.make_async_copy(meta_hbm.at[pl.ds(pl.multiple_of(blk * META_ROWS, META_ROWS), META_ROWS)],
                                     meta_smem.at[blk % 3], msem.at[blk % 3])

    def gather_start(blk, slot):
        ms = blk % 3
        for r in range(rows):
            src = pl.multiple_of(meta_smem[ms, TOK_ROW0 + r // V7X_LANES, r % V7X_LANES], n_chunk)
            pltpu.make_async_copy(h_hbm.at[pl.ds(src, n_chunk)], xbuf.at[slot, pl.ds(r * n_chunk, n_chunk)],
                                  xsem.at[slot]).start(priority=r % 2)

    def scatter_start(blk, slot, r0, r1):
        ms = blk % 3
        for r in range(r0, r1):
            dst = pl.multiple_of(meta_smem[ms, DST_ROW0 + r // V7X_LANES, r % V7X_LANES], n_chunk)
            pltpu.make_async_copy(ybuf.at[slot, pl.ds(r * n_chunk, n_chunk)], o_hbm.at[pl.ds(dst, n_chunk)],
                                  ysem.at[slot]).start(priority=r % 2)

    def gather_wait(slot):
        pltpu.make_async_copy(h_hbm.at[pl.ds(0, rows * n_chunk)], xbuf.at[slot], xsem.at[slot]).wait()

    def scatter_wait(slot):
        pltpu.make_async_copy(ybuf.at[slot], o_hbm.at[pl.ds(0, rows * n_chunk)], ysem.at[slot]).wait()

    slot = i % 2

    @pl.when(i < n)
    def _():
        @pl.when(i == 0)
        def _():
            meta_copy(0).start()
            ybuf[0] = jnp.zeros(ybuf.shape[1:], F32)
            n_spill = o_hbm.shape[0] - first_spill * n_chunk
            fills = [pltpu.make_async_copy(ybuf.at[0, pl.ds(0, min(rows * n_chunk, n_spill - s))],
                                           o_hbm.at[pl.ds(first_spill * n_chunk + s, min(rows * n_chunk, n_spill - s))],
                                           ysem.at[0])
                     for s in range(0, n_spill, rows * n_chunk)]
            for f in fills:
                f.start()
            for f in fills:
                f.wait()
            meta_copy(0).wait()
            gather_start(0, 0)

            @pl.when(n > 1)
            def _():
                meta_copy(1).start()

        @pl.when(i + 2 < n)
        def _():
            meta_copy(i + 2).start()

        @pl.when(i + 1 < n)
        def _():
            meta_copy(i + 1).wait()

        @pl.when(i >= 2)
        def _():
            scatter_wait(slot)

        @pl.when(jnp.logical_or(i == 0, be_ref[i] != be_ref[jnp.maximum(i - 1, 0)]))
        def _():
            wgu_bf[...] = wgu_ref[0].astype(BF16)
            wd_bf[...] = wd_ref[0].astype(BF16)

        gather_start(jnp.minimum(i + 1, n - 1), 1 - slot)
        gather_wait(slot)
        x = jnp.concatenate([xbuf[slot, pl.ds(c, rows, stride=n_chunk), :] for c in range(n_chunk)],
                            axis=-1).astype(BF16)
        gu = jnp.dot(x, wgu_bf[...], preferred_element_type=F32) + bgu_ref[0]
        dff = gu.shape[-1] // 2
        g = jnp.minimum(gu[:, :dff], SWIGLU_LIMIT)
        u = jnp.clip(gu[:, dff:], -SWIGLU_LIMIT, SWIGLU_LIMIT)
        act = (g * jax.nn.sigmoid(SWIGLU_ALPHA * g) * (u + 1.0)).astype(BF16)
        y = jnp.dot(act, wd_bf[...], preferred_element_type=F32) + bd_ref[0]
        for c in range(n_chunk):
            ybuf[slot, pl.ds(c, rows, stride=n_chunk), :] = y[:, c * V7X_LANES:(c + 1) * V7X_LANES]
        scatter_start(i, slot, 0, rows)

        @pl.when(i == n - 1)
        def _():
            gather_wait(1 - slot)
            scatter_wait(slot)

            @pl.when(n > 1)
            def _():
                scatter_wait(1 - slot)


def _moe(block_e, meta, h_rows, n_out_rows, wgu, bgu, wd, bd):
    n_blocks = block_e.shape[0] - 1
    d = wgu.shape[1]
    lanes = h_rows.shape[1]
    n_chunk = d // lanes
    any_spec = pl.BlockSpec(memory_space=pl.ANY)
    grid_spec = pltpu.PrefetchScalarGridSpec(
        num_scalar_prefetch=1,
        grid=(n_blocks,),
        in_specs=[any_spec, any_spec,
                  pl.BlockSpec((1, d, wgu.shape[2]), lambda i, be: (be[i], 0, 0)),
                  pl.BlockSpec((1, 1, bgu.shape[2]), lambda i, be: (be[i], 0, 0)),
                  pl.BlockSpec((1, wd.shape[1], d), lambda i, be: (be[i], 0, 0)),
                  pl.BlockSpec((1, 1, d), lambda i, be: (be[i], 0, 0))],
        out_specs=any_spec,
        scratch_shapes=[pltpu.SMEM((3, META_ROWS, V7X_LANES), jnp.int32), pltpu.SemaphoreType.DMA((3,)),
                        pltpu.VMEM((2, MOE_BLOCK * n_chunk, lanes), F32), pltpu.SemaphoreType.DMA((2,)),
                        pltpu.VMEM((2, MOE_BLOCK * n_chunk, lanes), F32), pltpu.SemaphoreType.DMA((2,)),
                        pltpu.VMEM(wgu.shape[1:], BF16), pltpu.VMEM(wd.shape[1:], BF16)],
    )
    first_spill = TOP_K * (h_rows.shape[0] // n_chunk)
    return pl.pallas_call(
        functools.partial(_moe_kernel, first_spill),
        grid_spec=grid_spec,
        out_shape=jax.ShapeDtypeStruct((n_out_rows * n_chunk, lanes), F32),
        compiler_params=_cparams(("arbitrary",)),
        name="moe",
    )(block_e, meta * n_chunk, h_rows, wgu, bgu, wd, bd)


def _route(top_e, n_tok):
    nk = n_tok * TOP_K
    flat_e = top_e.reshape(nk)
    counts = jnp.sum((flat_e[None, :] == jnp.arange(N_EXPERTS, dtype=jnp.int32)[:, None]).astype(jnp.int32), axis=1)
    padded = (counts + MOE_BLOCK - 1) // MOE_BLOCK * MOE_BLOCK
    pad_end = jnp.cumsum(padded)
    n_blocks = (nk + N_EXPERTS * (MOE_BLOCK - 1) + MOE_BLOCK - 1) // MOE_BLOCK
    n_rows = n_blocks * MOE_BLOCK
    shift = (nk + 1).bit_length()
    assert N_EXPERTS < (1 << (31 - shift))
    pad_cum = jnp.cumsum(padded - counts)
    pad_e = jnp.sum((pad_cum[None, :] <= jnp.arange(n_rows - nk, dtype=jnp.int32)[:, None]).astype(jnp.int32), axis=1)
    packed = jnp.concatenate([(flat_e << shift) + jnp.arange(1, nk + 1, dtype=jnp.int32), pad_e << shift])
    row_asg = (jnp.sort(packed) & ((1 << shift) - 1)) - 1
    is_real = row_asg >= 0
    row_tok = jnp.where(is_real, row_asg % n_tok, 0)
    spill = nk + jnp.cumsum(jnp.logical_not(is_real).astype(jnp.int32)) - 1
    row_dst = jnp.where(is_real, row_asg, spill)
    blk_start = jnp.arange(n_blocks, dtype=jnp.int32) * MOE_BLOCK
    block_e = jnp.sum((pad_end[None, :] <= blk_start[:, None]).astype(jnp.int32), axis=1)
    n_used = pad_end[N_EXPERTS - 1] // MOE_BLOCK
    last_e = jnp.max(jnp.where(counts > 0, jnp.arange(N_EXPERTS, dtype=jnp.int32), 0))
    block_e = jnp.concatenate([jnp.minimum(block_e, last_e), n_used[None]]).astype(jnp.int32)
    per = MOE_BLOCK // V7X_LANES
    meta = jnp.concatenate([row_tok.reshape(n_blocks, per, V7X_LANES), row_dst.reshape(n_blocks, per, V7X_LANES),
                            jnp.zeros((n_blocks, META_ROWS - 2 * per, V7X_LANES), jnp.int32)], axis=1)
    return block_e, meta.reshape(n_blocks * META_ROWS, V7X_LANES), n_rows


def _combine_kernel(x1_ref, gate_ref, o0_ref, o1_ref, o2_ref, o3_ref, y_ref):
    gate = gate_ref[...]
    tm, d = x1_ref.shape
    n_chunk = d // V7X_LANES
    cols = [gate[:, k:k + 1] for k in range(TOP_K)]
    for c in range(n_chunk):
        sl = slice(c * V7X_LANES, (c + 1) * V7X_LANES)
        acc = x1_ref[:, sl]
        for k, o_ref in enumerate((o0_ref, o1_ref, o2_ref, o3_ref)):
            acc = acc + cols[k] * o_ref[pl.ds(c, tm, stride=n_chunk), :]
        y_ref[:, sl] = acc


def _combine(x1, gate_col, o_rows, tm):
    n, d = x1.shape
    lanes = o_rows.shape[1]
    n_chunk = d // lanes
    assert n % tm == 0
    per = n // tm
    o_spec = lambda k: pl.BlockSpec((tm * n_chunk, lanes), lambda i, k=k: (k * per + i, 0))
    return pl.pallas_call(
        _combine_kernel,
        grid=(per,),
        in_specs=[pl.BlockSpec((tm, d), lambda i: (i, 0)), pl.BlockSpec((tm, V7X_LANES), lambda i: (i, 0))]
                 + [o_spec(k) for k in range(TOP_K)],
        out_specs=pl.BlockSpec((tm, d), lambda i: (i, 0)),
        out_shape=jax.ShapeDtypeStruct((n, d), F32),
        compiler_params=_cparams(("parallel",)),
        name="combine",
    )(x1, gate_col, o_rows, o_rows, o_rows, o_rows)


def _prep_weights(p):
    cw = p["w_dw"].shape[-1]
    o_gate = 2 * cw + 3 * MOBA_WIDTH + MEM_WIDTH
    row = lambda a: a.reshape(1, -1).astype(F32)
    tile = lambda g, n: jnp.tile(g.astype(F32), n).reshape(1, -1)
    prep = {}
    prep["proj"] = (
        row(p["g_norm1"]), p["w_in"][:, :o_gate].astype(BF16),
        _segment_ones(MOBA_WIDTH, MOBA_HEAD_DIM), _segment_ones(MEM_WIDTH, MEM_HEAD_DIM),
        tile(p["g_q_moba"], MOBA_HEADS), tile(p["g_k_moba"], MOBA_HEADS), tile(p["g_q_mem"], MEM_HEADS),
        p["w_dw"].astype(F32), row(p["b_dw"]), row(p["g_conv_ln"]), row(p["b_conv_ln"]),
    )
    prep["mem_kv"] = (row(p["g_mem_norm"]), p["w_mem_kv"].astype(BF16), _segment_ones(MEM_WIDTH, MEM_HEAD_DIM),
                      tile(p["g_k_mem"], MEM_HEADS))
    wr_t = p["w_router"].astype(F32).T
    wr_hi = wr_t.astype(BF16)
    wr_lo = (wr_t - wr_hi.astype(F32)).astype(BF16)
    prep["merge"] = (
        row(p["g_norm1"]), p["w_in"][:, o_gate:].astype(BF16), p["w_conv_out"].astype(BF16), row(p["b_conv_out"]),
        p["w_moba_o"].astype(BF16), p["w_mem_o"].astype(BF16), p["w_out"].astype(BF16), row(p["g_norm2"]),
        wr_hi, wr_lo, p["b_router"].astype(F32).reshape(-1, 1),
    )
    prep["moe"] = (p["w_gu"].astype(F32), p["b_gu"].astype(F32)[:, None, :], p["w_down"].astype(F32),
                   p["b_down"].astype(F32)[:, None, :])
    return prep


def _ffn(x2d, c, om, omem, tm, prep):
    n = x2d.shape[0]
    x1, h_rows, top_e, gate_col = _merge(x2d, c, om, omem, tm, prep["merge"])
    block_e, meta, n_rows = _route(top_e, n)
    o_rows = _moe(block_e, meta, h_rows, n_rows, *prep["moe"])
    return _combine(x1, gate_col, o_rows, tm)


_PARAM_NAMES = ("g_norm1", "w_in", "w_dw", "b_dw", "g_conv_ln", "b_conv_ln", "w_conv_out", "b_conv_out", "g_q_moba",
                "g_k_moba", "w_moba_o", "g_mem_norm", "w_mem_kv", "g_q_mem", "g_k_mem", "w_mem_o", "w_out", "g_norm2",
                "w_router", "b_router", "w_gu", "b_gu", "w_down", "b_down")

PROMPT_ROWS = 512
SAMPLE_BATCH_TILE = 64
SAMPLE_MEM_TILE = 8


def kernel(x_prompt, x_sample, mem_prompt, cache_moba_k, cache_moba_v, cache_mem_k, cache_mem_v, state_conv, page_table, g_norm1, w_in, w_dw, b_dw, g_conv_ln, b_conv_ln, w_conv_out, b_conv_out, g_q_moba, g_k_moba, w_moba_o, g_mem_norm, w_mem_kv, g_q_mem, g_k_mem, w_mem_o, w_out, g_norm2, w_router, b_router, w_gu, b_gu, w_down, b_down):
    params = dict(zip(_PARAM_NAMES, (g_norm1, w_in, w_dw, b_dw, g_conv_ln, b_conv_ln, w_conv_out, b_conv_out,
                                     g_q_moba, g_k_moba, w_moba_o, g_mem_norm, w_mem_kv, g_q_mem, g_k_mem, w_mem_o,
                                     w_out, g_norm2, w_router, b_router, w_gu, b_gu, w_down, b_down)))
    depth = g_norm1.shape[0]
    bp, lp, d = x_prompt.shape
    bs, ls, _ = x_sample.shape
    n_mem = mem_prompt.shape[1]
    y_p, y_s = x_prompt, x_sample
    outs = [[] for _ in range(8)]
    for layer in range(depth):
        prep = _prep_weights({k: v[layer] for k, v in params.items()})
        k, v, q, kb, vb, qm, c, tail = _proj(y_p, None, 1, PROMPT_ROWS, *prep["proj"])
        mk, mv = _mem_kv(mem_prompt.reshape(bp * n_mem, d), PROMPT_ROWS, *prep["mem_kv"])
        o_moba = _moba_prompt(q, kb, vb)
        o_mem = _mem_attend(qm, mk.reshape(1, bp, n_mem, MEM_WIDTH), mv.reshape(1, bp, n_mem, MEM_WIDTH), 0, 1,
                            PROMPT_ROWS)
        flat = lambda a: a.reshape(bp * lp, a.shape[-1])
        y_p = _ffn(flat(y_p), flat(c), flat(o_moba), flat(o_mem), PROMPT_ROWS, prep).reshape(bp, lp, d)
        mem_shape = (bp, n_mem, MEM_HEADS, MEM_HEAD_DIM)
        for lst, val in zip(outs[:5], (k, v, mk.reshape(mem_shape), mv.reshape(mem_shape), tail)):
            lst.append(val)
        k, v, q, kb, vb, qm, c, tail = _proj(y_s, state_conv[layer], SAMPLE_BATCH_TILE, ls, *prep["proj"])
        o_moba = _moba_sample(q, kb, vb, cache_moba_k, cache_moba_v, page_table, layer)
        o_mem = _mem_attend(qm, cache_mem_k.reshape(depth, bs, n_mem, MEM_WIDTH),
                            cache_mem_v.reshape(depth, bs, n_mem, MEM_WIDTH), layer, SAMPLE_MEM_TILE, ls)
        flat = lambda a: a.reshape(bs * ls, a.shape[-1])
        y_s = _ffn(flat(y_s), flat(c), flat(o_moba), flat(o_mem), PROMPT_ROWS, prep).reshape(bs, ls, d)
        for lst, val in zip(outs[5:], (k, v, tail)):
            lst.append(val)
    kp, vp, mkp, mvp, cp, ks, vs, cs = (jnp.stack(o) for o in outs)
    return (y_p, y_s, kp, vp, mkp, mvp, cp, ks, vs, cs)
```

```python
import functools

import jax
import jax.numpy as jnp
from jax import lax
from jax.experimental import pallas as pl
from jax.experimental.pallas import tpu as pltpu

F32 = jnp.float32
BF16 = jnp.bfloat16

EPS = 1e-6
MOBA_HEADS = 8
MOBA_HEAD_DIM = 64
MOBA_WIDTH = MOBA_HEADS * MOBA_HEAD_DIM
MOBA_BLOCK = 256
MOBA_TOPK = 3
MEM_HEADS = 4
MEM_HEAD_DIM = 128
MEM_WIDTH = MEM_HEADS * MEM_HEAD_DIM
CONV_KERNEL = 31
CONV_HIST = CONV_KERNEL - 1
N_EXPERTS = 32
TOP_K = 4
SWIGLU_LIMIT = 7.0
SWIGLU_ALPHA = 1.702
MOE_BLOCK = 256

V7X_LANES = 128
V7X_SUBLANES = 8
HIST_PAD = 32
VMEM_LIMIT = 56 * 1024 * 1024


def _cparams(sem):
    return pltpu.CompilerParams(dimension_semantics=sem, vmem_limit_bytes=VMEM_LIMIT)


def _full(shape):
    n = len(shape)
    return pl.BlockSpec(shape, lambda *_: (0,) * n)


def _segment_ones(width, seg):
    r = jnp.arange(width) // seg
    return (r[:, None] == r[None, :]).astype(BF16)


def _rms(x, g):
    return x * lax.rsqrt(jnp.mean(x * x, axis=-1, keepdims=True) + EPS) * g


def _seg_rms(z, seg_ones, seg, g):
    sq = (z * z).astype(BF16)
    parts = []
    for c in range(0, z.shape[-1], 2 * V7X_LANES):
        parts.append(jnp.dot(sq[:, c:c + 2 * V7X_LANES], seg_ones[c:c + 2 * V7X_LANES, c:c + 2 * V7X_LANES],
                             preferred_element_type=F32))
    ms = jnp.concatenate(parts, axis=-1) * (1.0 / seg)
    return z * lax.rsqrt(ms + EPS) * g


def _proj_kernel(has_hist, carry, x_ref, *refs):
    if has_hist:
        hist_ref, refs = refs[0], refs[1:]
    (g1_ref, w1_ref, seg64_ref, seg128_ref, gq_ref, gk_ref, gqm_ref, wdw_ref, bdw_ref, gln_ref, bln_ref,
     k_out, v_out, q_out, kb_out, vb_out, qm_out, c_out, tail_out, xp_scr) = refs
    bt, tl, d = x_ref.shape
    rows = bt * tl
    cw = c_out.shape[-1]
    t = pl.program_id(1)

    x = x_ref[...].reshape(rows, d)
    h = _rms(x, g1_ref[...]).astype(BF16)
    z = jnp.dot(h, w1_ref[...], preferred_element_type=F32)
    o_q = 2 * cw
    o_k = o_q + MOBA_WIDTH
    o_v = o_k + MOBA_WIDTH
    o_qm = o_v + MOBA_WIDTH
    u = z[:, :cw] * jax.nn.sigmoid(z[:, cw:o_q])
    seg64 = seg64_ref[...]
    qn = _seg_rms(z[:, o_q:o_k], seg64, MOBA_HEAD_DIM, gq_ref[...])
    kn = _seg_rms(z[:, o_k:o_v], seg64, MOBA_HEAD_DIM, gk_ref[...])
    vv = z[:, o_v:o_qm]
    qmn = _seg_rms(z[:, o_qm:o_qm + MEM_WIDTH], seg128_ref[...], MEM_HEAD_DIM, gqm_ref[...])

    q_out[...] = qn.astype(BF16).reshape(bt, tl, MOBA_WIDTH)
    kb_out[...] = kn.astype(BF16).reshape(bt, tl, MOBA_WIDTH)
    vb_out[...] = vv.astype(BF16).reshape(bt, tl, MOBA_WIDTH)
    qm_out[...] = qmn.astype(BF16).reshape(bt, tl, MEM_WIDTH)
    kn3 = kn.reshape(bt, tl, MOBA_WIDTH)
    vv3 = vv.reshape(bt, tl, MOBA_WIDTH)
    for hh in range(MOBA_HEADS):
        sl = slice(hh * MOBA_HEAD_DIM, (hh + 1) * MOBA_HEAD_DIM)
        k_out[:, hh, :, :] = kn3[:, :, sl]
        v_out[:, hh, :, :] = vv3[:, :, sl]

    lo = HIST_PAD - CONV_HIST

    @pl.when(t == 0)
    def _():
        xp_scr[:, 0:HIST_PAD, :] = jnp.zeros((bt, HIST_PAD, cw), F32)
        if has_hist:
            xp_scr[:, lo:HIST_PAD, :] = hist_ref[...]

    xp_scr[:, HIST_PAD:HIST_PAD + tl, :] = u.reshape(bt, tl, cw)
    rc = min(tl, 64)
    bdw = bdw_ref[...]
    gln = gln_ref[...]
    bln = bln_ref[...]
    for r0 in range(0, tl, rc):
        window = xp_scr[:, r0:r0 + rc + HIST_PAD, :]
        acc = jnp.zeros((bt, rc, cw), F32) + bdw
        for s in range(V7X_SUBLANES):
            ext = rc if s == 0 else rc + V7X_SUBLANES
            part = None
            for q in range(HIST_PAD // V7X_SUBLANES + 1):
                j = V7X_SUBLANES * q + s - lo
                if 0 <= j < CONV_KERNEL:
                    term = wdw_ref[j:j + 1, :] * window[:, V7X_SUBLANES * q:V7X_SUBLANES * q + ext, :]
                    part = term if part is None else part + term
            acc = acc + (part if s == 0 else part[:, s:s + rc, :])
        mu = jnp.mean(acc, axis=-1, keepdims=True)
        dev = acc - mu
        var = jnp.mean(dev * dev, axis=-1, keepdims=True)
        y = dev * lax.rsqrt(var + EPS) * gln + bln
        c_out[:, r0:r0 + rc, :] = (y * jax.nn.sigmoid(y)).astype(BF16)
    tail_out[...] = xp_scr[:, tl + lo:tl + HIST_PAD, :]
    if carry:
        xp_scr[:, 0:HIST_PAD, :] = xp_scr[:, tl:tl + HIST_PAD, :]


def _proj(x, hist, bt, tl, g1, w1, seg64, seg128, gq, gk, gqm, wdw, bdw, gln, bln):
    b, l, d = x.shape
    cw = wdw.shape[-1]
    n_t = l // tl
    has_hist = hist is not None
    carry = n_t > 1
    assert b % bt == 0 and l % tl == 0 and (not carry or (tl >= HIST_PAD and bt == 1))
    tok = lambda w: pl.BlockSpec((bt, tl, w), lambda i, j: (i, j, 0))
    head = pl.BlockSpec((bt, MOBA_HEADS, tl, MOBA_HEAD_DIM), lambda i, j: (i, 0, j, 0))
    in_specs = [tok(d)]
    args = [x]
    if has_hist:
        in_specs.append(pl.BlockSpec((bt, CONV_HIST, cw), lambda i, j: (i, 0, 0)))
        args.append(hist)
    consts = [g1, w1, seg64, seg128, gq, gk, gqm, wdw, bdw, gln, bln]
    in_specs += [_full(c.shape) for c in consts]
    out_shape = [
        jax.ShapeDtypeStruct((b, MOBA_HEADS, l, MOBA_HEAD_DIM), F32),
        jax.ShapeDtypeStruct((b, MOBA_HEADS, l, MOBA_HEAD_DIM), F32),
        jax.ShapeDtypeStruct((b, l, MOBA_WIDTH), BF16),
        jax.ShapeDtypeStruct((b, l, MOBA_WIDTH), BF16),
        jax.ShapeDtypeStruct((b, l, MOBA_WIDTH), BF16),
        jax.ShapeDtypeStruct((b, l, MEM_WIDTH), BF16),
        jax.ShapeDtypeStruct((b, l, cw), BF16),
        jax.ShapeDtypeStruct((b, CONV_HIST, cw), F32),
    ]
    out_specs = [head, head, tok(MOBA_WIDTH), tok(MOBA_WIDTH), tok(MOBA_WIDTH), tok(MEM_WIDTH), tok(cw),
                 pl.BlockSpec((bt, CONV_HIST, cw), lambda i, j: (i, 0, 0))]
    return pl.pallas_call(
        functools.partial(_proj_kernel, has_hist, carry),
        grid=(b // bt, n_t),
        in_specs=in_specs,
        out_specs=out_specs,
        out_shape=out_shape,
        scratch_shapes=[pltpu.VMEM((bt, HIST_PAD + tl, cw), F32)],
        compiler_params=_cparams(("parallel", "arbitrary")),
        name="proj",
    )(*args, *consts)


def _memkv_kernel(x_ref, g_ref, w_ref, seg128_ref, gk_ref, k_out, v_out):
    h = _rms(x_ref[...], g_ref[...]).astype(BF16)
    kv = jnp.dot(h, w_ref[...], preferred_element_type=F32)
    k_out[...] = _seg_rms(kv[:, :MEM_WIDTH], seg128_ref[...], MEM_HEAD_DIM, gk_ref[...])
    v_out[...] = kv[:, MEM_WIDTH:]


def _mem_kv(mem2d, tm, g, w, seg128, gk):
    n, d = mem2d.shape
    assert n % tm == 0
    spec = pl.BlockSpec((tm, MEM_WIDTH), lambda i: (i, 0))
    return pl.pallas_call(
        _memkv_kernel,
        grid=(n // tm,),
        in_specs=[pl.BlockSpec((tm, d), lambda i: (i, 0))] + [_full(c.shape) for c in (g, w, seg128, gk)],
        out_specs=[spec, spec],
        out_shape=[jax.ShapeDtypeStruct((n, MEM_WIDTH), F32)] * 2,
        compiler_params=_cparams(("parallel",)),
        name="mem_kv",
    )(mem2d, g, w, seg128, gk)


def _mematt_kernel(q_ref, k_ref, v_ref, o_ref):
    scale = MEM_HEAD_DIM ** -0.5
    for hh in range(MEM_HEADS):
        sl = slice(hh * MEM_HEAD_DIM, (hh + 1) * MEM_HEAD_DIM)
        q = q_ref[:, :, sl]
        if len(k_ref.shape) == 5:
            k = k_ref[0, :, :, hh, :].astype(BF16)
            v = v_ref[0, :, :, hh, :].astype(BF16)
        else:
            k = k_ref[0, :, :, sl].astype(BF16)
            v = v_ref[0, :, :, sl].astype(BF16)
        s = jnp.einsum("bqd,bkd->bqk", q, k, preferred_element_type=F32) * scale
        m = jnp.max(s, axis=-1, keepdims=True)
        p = jnp.exp(s - m)
        l = jnp.sum(p, axis=-1, keepdims=True)
        o = jnp.einsum("bqk,bkd->bqd", p.astype(BF16), v, preferred_element_type=F32)
        o_ref[:, :, sl] = (o / l).astype(BF16)


def _mem_attend(qm, mk, mv, layer, bt, tq):
    b, l, w = qm.shape
    m = mk.shape[2]
    assert b % bt == 0 and l % tq == 0 and mk.shape[3:] in ((MEM_HEADS, MEM_HEAD_DIM), (MEM_WIDTH,))
    tail = mk.shape[3:]
    kv_spec = pl.BlockSpec((1, bt, m) + tail, lambda i, j: (layer, i, 0) + (0,) * len(tail))
    return pl.pallas_call(
        _mematt_kernel,
        grid=(b // bt, l // tq),
        in_specs=[pl.BlockSpec((bt, tq, w), lambda i, j: (i, j, 0)), kv_spec, kv_spec],
        out_specs=pl.BlockSpec((bt, tq, w), lambda i, j: (i, j, 0)),
        out_shape=jax.ShapeDtypeStruct((b, l, w), BF16),
        compiler_params=_cparams(("parallel", "arbitrary")),
        name="mem_attend",
    )(qm, mk, mv)


def _moba_prompt_kernel(q_ref, k_ref, v_ref, o_ref, vt_scr, km_scr, sel_scr, acc_scr, m_scr, l_scr, wq_scr):
    blk = MOBA_BLOCK
    l = k_ref.shape[1]
    nb = l // blk
    qb = pl.program_id(1)
    hd = MOBA_HEAD_DIM

    @pl.when(qb == 0)
    def _():
        lane_head = lax.broadcasted_iota(jnp.int32, (MOBA_HEADS, MOBA_WIDTH), 1) // hd
        row_head = lax.broadcasted_iota(jnp.int32, (MOBA_HEADS, MOBA_WIDTH), 0)
        head_mask = (lane_head == row_head).astype(F32)
        for n in range(nb):
            vt_scr[n] = v_ref[0, n * blk:(n + 1) * blk, :].T
            kmean = jnp.mean(k_ref[0, n * blk:(n + 1) * blk, :].astype(F32), axis=0, keepdims=True)
            km_scr[n * MOBA_HEADS:(n + 1) * MOBA_HEADS, :] = (kmean * head_mask).astype(BF16)

    qt = q_ref[0].T
    gate = jnp.dot(km_scr[...], qt, preferred_element_type=F32).reshape(nb, MOBA_HEADS, blk)
    for n in range(nb):
        rank = jnp.zeros((MOBA_HEADS, blk), jnp.int32)
        for n2 in range(nb):
            if n2 == n:
                continue
            beats = (gate[n2] >= gate[n]) if n2 < n else (gate[n2] > gate[n])
            rank = rank + jnp.where(jnp.logical_and(beats, n2 < qb), 1, 0)
        sel_scr[n] = jnp.where(jnp.logical_and(rank < MOBA_TOPK, n < qb), 1.0, 0.0)

    pairs = MOBA_HEADS // 2
    row_in_pair = lax.broadcasted_iota(jnp.int32, (2 * hd, blk), 0) // hd
    for pr in range(pairs):
        qp = qt[pr * 2 * hd:(pr + 1) * 2 * hd, :] * jnp.asarray(hd ** -0.5, BF16)
        zero = jnp.zeros_like(qp)
        wq_scr[pr] = jnp.concatenate([jnp.where(row_in_pair == 0, qp, zero), jnp.where(row_in_pair == 1, qp, zero)],
                                     axis=1)

    def attend(kblk, n, keep, first):
        s_pairs = [jnp.dot(kblk[:, pr * 2 * hd:(pr + 1) * 2 * hd], wq_scr[pr], preferred_element_type=F32)
                   for pr in range(pairs)]
        m_old = None if first else m_scr[...]
        l_old = None if first else l_scr[...]
        m_rows, l_rows, alphas, probs = [], [], [], []
        for hh in range(MOBA_HEADS):
            s = jnp.where(keep(hh), s_pairs[hh // 2][:, (hh % 2) * blk:(hh % 2 + 1) * blk], -jnp.inf)
            m_new = jnp.max(s, axis=0, keepdims=True)
            if not first:
                m_new = jnp.maximum(m_old[hh:hh + 1, :], m_new)
                alphas.append(jnp.exp(m_old[hh:hh + 1, :] - m_new))
            p = jnp.exp(s - m_new)
            p_sum = jnp.sum(p, axis=0, keepdims=True)
            l_rows.append(p_sum if first else alphas[hh] * l_old[hh:hh + 1, :] + p_sum)
            m_rows.append(m_new)
            probs.append(p.astype(BF16))
        m_scr[...] = jnp.concatenate(m_rows, axis=0)
        l_scr[...] = jnp.concatenate(l_rows, axis=0)
        pvs = [jnp.dot(vt_scr[n, hh * hd:(hh + 1) * hd, :], probs[hh], preferred_element_type=F32)
               for hh in range(MOBA_HEADS)]
        for hh in range(MOBA_HEADS):
            rows = slice(hh * hd, (hh + 1) * hd)
            acc_scr[rows, :] = pvs[hh] if first else alphas[hh] * acc_scr[rows, :] + pvs[hh]

    causal = lax.broadcasted_iota(jnp.int32, (blk, blk), 0) <= lax.broadcasted_iota(jnp.int32, (blk, blk), 1)
    attend(k_ref[0, pl.ds(pl.multiple_of(qb * blk, blk), blk), :], qb, lambda hh: causal, True)

    def past_block(n, carry):
        sel = sel_scr[n]
        attend(k_ref[0, pl.ds(pl.multiple_of(n * blk, blk), blk), :], n, lambda hh: sel[hh:hh + 1, :] > 0.0, False)
        return carry

    lax.fori_loop(0, qb, past_block, 0)

    for hh in range(MOBA_HEADS):
        acc_scr[hh * hd:(hh + 1) * hd, :] = acc_scr[hh * hd:(hh + 1) * hd, :] / l_scr[hh:hh + 1, :]
    o_ref[0] = acc_scr[...].T.astype(BF16)


def _moba_prompt(q, kb, vb):
    b, l, w = q.shape
    blk = MOBA_BLOCK
    assert l % blk == 0
    nb = l // blk
    full = pl.BlockSpec((1, l, w), lambda i, j: (i, 0, 0))
    tile = pl.BlockSpec((1, blk, w), lambda i, j: (i, j, 0))
    return pl.pallas_call(
        _moba_prompt_kernel,
        grid=(b, nb),
        in_specs=[tile, full, full],
        out_specs=tile,
        out_shape=jax.ShapeDtypeStruct((b, l, w), BF16),
        scratch_shapes=[pltpu.VMEM((nb, w, blk), BF16), pltpu.VMEM((nb * MOBA_HEADS, w), BF16),
                        pltpu.VMEM((nb, MOBA_HEADS, blk), F32), pltpu.VMEM((w, blk), F32),
                        pltpu.VMEM((MOBA_HEADS, blk), F32), pltpu.VMEM((MOBA_HEADS, blk), F32),
                        pltpu.VMEM((MOBA_HEADS // 2, 2 * MOBA_HEAD_DIM, 2 * blk), BF16)],
        compiler_params=_cparams(("parallel", "arbitrary")),
        name="moba_prompt",
    )(q, kb, vb)


SAMPLE_GROUP = 4
SAMPLE_RING = 3


def _moba_sample_kernel(layer, pt_ref, q_ref, kn_ref, vn_ref, ck_hbm, cv_hbm, o_ref,
                        kbuf, vbuf, ksem, vsem, m_scr, l_scr, g_scr, o_scr):
    b = pl.program_id(0)
    nb = pl.num_programs(0)
    n_pages = pt_ref.shape[1]
    page = kbuf.shape[4]
    ppb = MOBA_BLOCK // page
    n_full = n_pages // ppb
    grp = SAMPLE_GROUP
    n_grp = n_full // grp
    hd = MOBA_HEAD_DIM
    t = q_ref.shape[1]
    ncol = MOBA_HEADS * t
    scale = hd ** -0.5
    nt = (((1,), (1,)), ((), ()))

    def copies(bb, g, par):
        out = []
        for j in range(grp):
            for pg in range(ppb):
                pid = pt_ref[bb, (g * grp + j) * ppb + pg]
                out.append(pltpu.make_async_copy(ck_hbm.at[layer, pid], kbuf.at[par, j * ppb + pg], ksem.at[par]))
                out.append(pltpu.make_async_copy(cv_hbm.at[layer, pid], vbuf.at[par, j * ppb + pg], vsem.at[par]))
        return out

    ahead = SAMPLE_RING - 1

    @pl.when(b == 0)
    def _():
        for f in range(min(ahead, pt_ref.shape[0] * n_grp)):
            for c in copies(f // n_grp, f % n_grp, f % SAMPLE_RING):
                c.start()

    q = q_ref[0]
    row_head = lax.broadcasted_iota(jnp.int32, (V7X_LANES, MOBA_WIDTH), 0) // t
    lane_head = lax.broadcasted_iota(jnp.int32, (V7X_LANES, MOBA_WIDTH), 1) // hd
    q_tiled = jnp.concatenate([q.astype(F32)] * (V7X_LANES // t), axis=0)
    qrows = jnp.where(row_head == lane_head, q_tiled, 0.0).astype(BF16)

    def packed(buf, par, j):
        return jnp.concatenate([buf[par, j * ppb + pg].reshape(MOBA_WIDTH, page) for pg in range(ppb)],
                               axis=-1).astype(BF16)

    lane_id = lax.broadcasted_iota(jnp.int32, (V7X_LANES, V7X_LANES), 1)
    g_scr[...] = jnp.full((V7X_LANES, V7X_LANES), -jnp.inf, F32)
    m_scr[...] = jnp.full((V7X_LANES, V7X_LANES), -jnp.inf, F32)
    l_scr[...] = jnp.zeros((V7X_LANES, V7X_LANES), F32)

    def group(g, carry):
        flat = b * n_grp + g
        par = flat % SAMPLE_RING
        nxt = flat + ahead

        @pl.when(nxt < nb * n_grp)
        def _():
            for c in copies(nxt // n_grp, nxt % n_grp, nxt % SAMPLE_RING):
                c.start()

        for c in copies(b, g, par):
            c.wait()
        for j in range(grp):
            n = g * grp + j
            s_raw = jnp.dot(qrows, packed(kbuf, par, j), preferred_element_type=F32)
            s = s_raw * scale
            m = jnp.max(s, axis=-1, keepdims=True)
            p = jnp.exp(s - m)
            hit = lane_id == n
            g_scr[...] = jnp.where(hit, jnp.sum(s_raw, axis=-1, keepdims=True) * (1.0 / MOBA_BLOCK), g_scr[...])
            m_scr[...] = jnp.where(hit, m, m_scr[...])
            l_scr[...] = jnp.where(hit, jnp.sum(p, axis=-1, keepdims=True), l_scr[...])
            o_scr[n] = lax.dot_general(p[:ncol, :].astype(BF16), packed(vbuf, par, j), nt, preferred_element_type=F32)
        return carry

    lax.fori_loop(0, n_grp, group, 0)

    gate = g_scr[...]
    sel = jnp.zeros(gate.shape, jnp.bool_)
    for _ in range(min(MOBA_TOPK, n_full)):
        mx = jnp.max(gate, axis=-1, keepdims=True)
        first = jnp.min(jnp.where(gate == mx, lane_id, V7X_LANES), axis=-1, keepdims=True)
        hit = lane_id == first
        sel = jnp.logical_or(sel, hit)
        gate = jnp.where(hit, -jnp.inf, gate)

    s_new = lax.dot_general(qrows, kn_ref[0], nt, preferred_element_type=F32) * scale
    row_tok = lax.broadcasted_iota(jnp.int32, s_new.shape, 0) % t
    key_tok = lax.broadcasted_iota(jnp.int32, s_new.shape, 1)
    s_new = jnp.where(key_tok <= row_tok, s_new, -jnp.inf)
    m_blk = jnp.where(sel, m_scr[...], -jnp.inf)
    m_fin = jnp.maximum(jnp.max(m_blk, axis=-1, keepdims=True), jnp.max(s_new, axis=-1, keepdims=True))
    w_blk = jnp.where(sel, jnp.exp(m_blk - m_fin), 0.0)
    p_new = jnp.exp(s_new - m_fin)
    l_fin = jnp.sum(w_blk * l_scr[...], axis=-1, keepdims=True) + jnp.sum(p_new, axis=-1, keepdims=True)
    o_fin = jnp.dot(p_new[:ncol, :].astype(BF16), vn_ref[0], preferred_element_type=F32)
    for n in range(n_full):
        o_fin = o_fin + w_blk[:ncol, n:n + 1] * o_scr[n]
    o_fin = o_fin / l_fin[:ncol, :]
    for hh in range(MOBA_HEADS):
        o_ref[0, :, hh * hd:(hh + 1) * hd] = o_fin[hh * t:(hh + 1) * t, hh * hd:(hh + 1) * hd].astype(BF16)


def _moba_sample(q, kn, vn, cache_k, cache_v, page_table, layer):
    b, t, w = q.shape
    n_pages = page_table.shape[1]
    _, _, heads, page, hd = cache_k.shape
    cache_k, cache_v = jnp.swapaxes(cache_k, 3, 4), jnp.swapaxes(cache_v, 3, 4)
    ppb = MOBA_BLOCK // page
    assert MOBA_BLOCK % page == 0 and n_pages % ppb == 0 and heads == MOBA_HEADS and hd == MOBA_HEAD_DIM
    n_full = n_pages // ppb
    assert n_full % SAMPLE_GROUP == 0 and MOBA_TOPK <= n_full <= V7X_LANES and heads * t <= V7X_LANES
    tok = pl.BlockSpec((1, t, w), lambda i, pt: (i, 0, 0))
    any_spec = pl.BlockSpec(memory_space=pl.ANY)
    pages = SAMPLE_GROUP * ppb
    grid_spec = pltpu.PrefetchScalarGridSpec(
        num_scalar_prefetch=1,
        grid=(b,),
        in_specs=[tok, tok, tok, any_spec, any_spec],
        out_specs=tok,
        scratch_shapes=[pltpu.VMEM((SAMPLE_RING, pages, heads, hd, page), F32),
                        pltpu.VMEM((SAMPLE_RING, pages, heads, hd, page), F32),
                        pltpu.SemaphoreType.DMA((SAMPLE_RING,)), pltpu.SemaphoreType.DMA((SAMPLE_RING,)),
                        pltpu.VMEM((V7X_LANES, V7X_LANES), F32), pltpu.VMEM((V7X_LANES, V7X_LANES), F32),
                        pltpu.VMEM((V7X_LANES, V7X_LANES), F32), pltpu.VMEM((n_full, heads * t, w), F32)],
    )
    return pl.pallas_call(
        functools.partial(_moba_sample_kernel, layer),
        grid_spec=grid_spec,
        out_shape=jax.ShapeDtypeStruct((b, t, w), BF16),
        compiler_params=_cparams(("arbitrary",)),
        name="moba_sample",
    )(page_table, q, kn, vn, cache_k, cache_v)


def _merge_kernel(x_ref, c_ref, om_ref, omem_ref, g1_ref, wg_ref, wco_ref, bco_ref, wmo_ref, wmemo_ref, wout_ref,
                  g2_ref, wr_hi_ref, wr_lo_ref, br_ref, x1_out, h2_out, e_out, gate_out):
    tm, d = x_ref.shape
    x = x_ref[...]
    h = _rms(x, g1_ref[...]).astype(BF16)
    gates = jax.nn.sigmoid(jnp.dot(h, wg_ref[...], preferred_element_type=F32))
    u_conv = jnp.dot(c_ref[...], wco_ref[...], preferred_element_type=F32) + bco_ref[...]
    u_moba = jnp.dot(om_ref[...], wmo_ref[...], preferred_element_type=F32)
    u_mem = jnp.dot(omem_ref[...], wmemo_ref[...], preferred_element_type=F32)
    merged = gates[:, :d] * u_conv + gates[:, d:2 * d] * u_moba + gates[:, 2 * d:] * u_mem
    x1 = x + jnp.dot(merged.astype(BF16), wout_ref[...], preferred_element_type=F32)
    x1_out[...] = x1
    h2 = _rms(x1, g2_ref[...])
    for c in range(d // V7X_LANES):
        h2_out[pl.ds(c, tm, stride=d // V7X_LANES), :] = h2[:, c * V7X_LANES:(c + 1) * V7X_LANES]

    h2_hi = h2.astype(BF16)
    h2_lo = (h2 - h2_hi.astype(F32)).astype(BF16)
    nt = (((1,), (1,)), ((), ()))
    logits = (lax.dot_general(wr_hi_ref[...], h2_hi, nt, preferred_element_type=F32)
              + lax.dot_general(wr_hi_ref[...], h2_lo, nt, preferred_element_type=F32)
              + lax.dot_general(wr_lo_ref[...], h2_hi, nt, preferred_element_type=F32)) + br_ref[...]
    e_id = lax.broadcasted_iota(jnp.int32, logits.shape, 0)
    tops, ids = [], []
    for _ in range(TOP_K):
        mx = jnp.max(logits, axis=0, keepdims=True)
        first = jnp.min(jnp.where(logits == mx, e_id, N_EXPERTS), axis=0, keepdims=True)
        tops.append(mx)
        ids.append(first)
        logits = jnp.where(e_id == first, -jnp.inf, logits)
    ex = [jnp.exp(v - tops[0]) for v in tops]
    denom = ex[0] + ex[1] + ex[2] + ex[3]
    e_out[...] = jnp.concatenate(ids, axis=0)
    gate_rows = jnp.concatenate([v / denom for v in ex] + [jnp.zeros((V7X_LANES - TOP_K, tm), F32)], axis=0)
    gate_out[...] = gate_rows.T


def _merge(x2d, c, om, omem, tm, consts):
    n, d = x2d.shape
    assert n % tm == 0
    row = lambda w: pl.BlockSpec((tm, w), lambda i: (i, 0))
    n_chunk = d // V7X_LANES
    return pl.pallas_call(
        _merge_kernel,
        grid=(n // tm,),
        in_specs=[row(d), row(c.shape[1]), row(om.shape[1]), row(omem.shape[1])] + [_full(a.shape) for a in consts],
        out_specs=[row(d), pl.BlockSpec((tm * n_chunk, V7X_LANES), lambda i: (i, 0)),
                   pl.BlockSpec((TOP_K, tm), lambda i: (0, i)), row(V7X_LANES)],
        out_shape=[jax.ShapeDtypeStruct((n, d), F32), jax.ShapeDtypeStruct((n * n_chunk, V7X_LANES), F32),
                   jax.ShapeDtypeStruct((TOP_K, n), jnp.int32), jax.ShapeDtypeStruct((n, V7X_LANES), F32)],
        compiler_params=_cparams(("parallel",)),
        name="merge",
    )(x2d, c, om, omem, *consts)


META_ROWS = V7X_SUBLANES
TOK_ROW0 = 0
DST_ROW0 = MOE_BLOCK // V7X_LANES


def _moe_kernel(first_spill, be_ref, meta_hbm, h_hbm, wgu_ref, bgu_ref, wd_ref, bd_ref, o_hbm,
                meta_smem, msem, xbuf, xsem, ybuf, ysem, wgu_bf, wd_bf):
    i = pl.program_id(0)
    n = be_ref[pl.num_programs(0)]
    rows = MOE_BLOCK
    n_chunk = xbuf.shape[1] // rows

    def meta_copy(blk):
        return pltpu.make_async_copy(meta_hbm.at[pl.ds(pl.multiple_of(blk * META_ROWS, META_ROWS), META_ROWS)],
                                     meta_smem.at[blk % 3], msem.at[blk % 3])

    def gather_start(blk, slot):
        ms = blk % 3
        for r in range(rows):
            src = pl.multiple_of(meta_smem[ms, TOK_ROW0 + r // V7X_LANES, r % V7X_LANES], n_chunk)
            pltpu.make_async_copy(h_hbm.at[pl.ds(src, n_chunk)], xbuf.at[slot, pl.ds(r * n_chunk, n_chunk)],
                                  xsem.at[slot]).start(priority=r % 2)

    def scatter_start(blk, slot, r0, r1):
        ms = blk % 3
        for r in range(r0, r1):
            dst = pl.multiple_of(meta_smem[ms, DST_ROW0 + r // V7X_LANES, r % V7X_LANES], n_chunk)
            pltpu.make_async_copy(ybuf.at[slot, pl.ds(r * n_chunk, n_chunk)], o_hbm.at[pl.ds(dst, n_chunk)],
                                  ysem.at[slot]).start(priority=r % 2)

    def gather_wait(slot):
        pltpu.make_async_copy(h_hbm.at[pl.ds(0, rows * n_chunk)], xbuf.at[slot], xsem.at[slot]).wait()

    def scatter_wait(slot):
        pltpu.make_async_copy(ybuf.at[slot], o_hbm.at[pl.ds(0, rows * n_chunk)], ysem.at[slot]).wait()

    slot = i % 2

    @pl.when(i < n)
    def _():
        @pl.when(i == 0)
        def _():
            meta_copy(0).start()
            ybuf[0] = jnp.zeros(ybuf.shape[1:], F32)
            n_spill = o_hbm.shape[0] - first_spill * n_chunk
            fills = [pltpu.make_async_copy(ybuf.at[0, pl.ds(0, min(rows * n_chunk, n_spill - s))],
                                           o_hbm.at[pl.ds(first_spill * n_chunk + s, min(rows * n_chunk, n_spill - s))],
                                           ysem.at[0])
                     for s in range(0, n_spill, rows * n_chunk)]
            for f in fills:
                f.start()
            for f in fills:
                f.wait()
            meta_copy(0).wait()
            gather_start(0, 0)

            @pl.when(n > 1)
            def _():
                meta_copy(1).start()

        @pl.when(i + 2 < n)
        def _():
            meta_copy(i + 2).start()

        @pl.when(i + 1 < n)
        def _():
            meta_copy(i + 1).wait()

        @pl.when(i >= 2)
        def _():
            scatter_wait(slot)

        @pl.when(jnp.logical_or(i == 0, be_ref[i] != be_ref[jnp.maximum(i - 1, 0)]))
        def _():
            wgu_bf[...] = wgu_ref[0].astype(BF16)
            wd_bf[...] = wd_ref[0].astype(BF16)

        gather_start(jnp.minimum(i + 1, n - 1), 1 - slot)
        gather_wait(slot)
        x = jnp.concatenate([xbuf[slot, pl.ds(c, rows, stride=n_chunk), :] for c in range(n_chunk)],
                            axis=-1).astype(BF16)
        gu = jnp.dot(x, wgu_bf[...], preferred_element_type=F32) + bgu_ref[0]
        dff = gu.shape[-1] // 2
        g = jnp.minimum(gu[:, :dff], SWIGLU_LIMIT)
        u = jnp.clip(gu[:, dff:], -SWIGLU_LIMIT, SWIGLU_LIMIT)
        act = (g * jax.nn.sigmoid(SWIGLU_ALPHA * g) * (u + 1.0)).astype(BF16)
        y = jnp.dot(act, wd_bf[...], preferred_element_type=F32) + bd_ref[0]
        for c in range(n_chunk):
            ybuf[slot, pl.ds(c, rows, stride=n_chunk), :] = y[:, c * V7X_LANES:(c + 1) * V7X_LANES]
        scatter_start(i, slot, 0, rows)

        @pl.when(i == n - 1)
        def _():
            gather_wait(1 - slot)
            scatter_wait(slot)

            @pl.when(n > 1)
            def _():
                scatter_wait(1 - slot)


def _moe(block_e, meta, h_rows, n_out_rows, wgu, bgu, wd, bd):
    n_blocks = block_e.shape[0] - 1
    d = wgu.shape[1]
    lanes = h_rows.shape[1]
    n_chunk = d // lanes
    any_spec = pl.BlockSpec(memory_space=pl.ANY)
    grid_spec = pltpu.PrefetchScalarGridSpec(
        num_scalar_prefetch=1,
        grid=(n_blocks,),
        in_specs=[any_spec, any_spec,
                  pl.BlockSpec((1, d, wgu.shape[2]), lambda i, be: (be[i], 0, 0)),
                  pl.BlockSpec((1, 1, bgu.shape[2]), lambda i, be: (be[i], 0, 0)),
                  pl.BlockSpec((1, wd.shape[1], d), lambda i, be: (be[i], 0, 0)),
                  pl.BlockSpec((1, 1, d), lambda i, be: (be[i], 0, 0))],
        out_specs=any_spec,
        scratch_shapes=[pltpu.SMEM((3, META_ROWS, V7X_LANES), jnp.int32), pltpu.SemaphoreType.DMA((3,)),
                        pltpu.VMEM((2, MOE_BLOCK * n_chunk, lanes), F32), pltpu.SemaphoreType.DMA((2,)),
                        pltpu.VMEM((2, MOE_BLOCK * n_chunk, lanes), F32), pltpu.SemaphoreType.DMA((2,)),
                        pltpu.VMEM(wgu.shape[1:], BF16), pltpu.VMEM(wd.shape[1:], BF16)],
    )
    first_spill = TOP_K * (h_rows.shape[0] // n_chunk)
    return pl.pallas_call(
        functools.partial(_moe_kernel, first_spill),
        grid_spec=grid_spec,
        out_shape=jax.ShapeDtypeStruct((n_out_rows * n_chunk, lanes), F32),
        compiler_params=_cparams(("arbitrary",)),
        name="moe",
    )(block_e, meta * n_chunk, h_rows, wgu, bgu, wd, bd)


def _route(top_e, n_tok):
    nk = n_tok * TOP_K
    flat_e = top_e.reshape(nk)
    counts = jnp.sum((flat_e[None, :] == jnp.arange(N_EXPERTS, dtype=jnp.int32)[:, None]).astype(jnp.int32), axis=1)
    padded = (counts + MOE_BLOCK - 1) // MOE_BLOCK * MOE_BLOCK
    pad_end = jnp.cumsum(padded)
    n_blocks = (nk + N_EXPERTS * (MOE_BLOCK - 1) + MOE_BLOCK - 1) // MOE_BLOCK
    n_rows = n_blocks * MOE_BLOCK
    shift = (nk + 1).bit_length()
    assert N_EXPERTS < (1 << (31 - shift))
    pad_cum = jnp.cumsum(padded - counts)
    pad_e = jnp.sum((pad_cum[None, :] <= jnp.arange(n_rows - nk, dtype=jnp.int32)[:, None]).astype(jnp.int32), axis=1)
    packed = jnp.concatenate([(flat_e << shift) + jnp.arange(1, nk + 1, dtype=jnp.int32), pad_e << shift])
    row_asg = (lax.sort(packed, is_stable=False) & ((1 << shift) - 1)) - 1
    is_real = row_asg >= 0
    row_tok = jnp.where(is_real, row_asg % n_tok, 0)
    spill = nk + jnp.cumsum(jnp.logical_not(is_real).astype(jnp.int32)) - 1
    row_dst = jnp.where(is_real, row_asg, spill)
    blk_start = jnp.arange(n_blocks, dtype=jnp.int32) * MOE_BLOCK
    block_e = jnp.sum((pad_end[None, :] <= blk_start[:, None]).astype(jnp.int32), axis=1)
    n_used = pad_end[N_EXPERTS - 1] // MOE_BLOCK
    last_e = jnp.max(jnp.where(counts > 0, jnp.arange(N_EXPERTS, dtype=jnp.int32), 0))
    block_e = jnp.concatenate([jnp.minimum(block_e, last_e), n_used[None]]).astype(jnp.int32)
    per = MOE_BLOCK // V7X_LANES
    meta = jnp.concatenate([row_tok.reshape(n_blocks, per, V7X_LANES), row_dst.reshape(n_blocks, per, V7X_LANES),
                            jnp.zeros((n_blocks, META_ROWS - 2 * per, V7X_LANES), jnp.int32)], axis=1)
    return block_e, meta.reshape(n_blocks * META_ROWS, V7X_LANES), n_rows


def _combine_kernel(x1_ref, gate_ref, o0_ref, o1_ref, o2_ref, o3_ref, y_ref):
    gate = gate_ref[...]
    tm, d = x1_ref.shape
    n_chunk = d // V7X_LANES
    cols = [gate[:, k:k + 1] for k in range(TOP_K)]
    for c in range(n_chunk):
        sl = slice(c * V7X_LANES, (c + 1) * V7X_LANES)
        acc = x1_ref[:, sl]
        for k, o_ref in enumerate((o0_ref, o1_ref, o2_ref, o3_ref)):
            acc = acc + cols[k] * o_ref[pl.ds(c, tm, stride=n_chunk), :]
        y_ref[:, sl] = acc


def _combine(x1, gate_col, o_rows, tm):
    n, d = x1.shape
    lanes = o_rows.shape[1]
    n_chunk = d // lanes
    assert n % tm == 0
    per = n // tm
    o_spec = lambda k: pl.BlockSpec((tm * n_chunk, lanes), lambda i, k=k: (k * per + i, 0))
    return pl.pallas_call(
        _combine_kernel,
        grid=(per,),
        in_specs=[pl.BlockSpec((tm, d), lambda i: (i, 0)), pl.BlockSpec((tm, V7X_LANES), lambda i: (i, 0))]
                 + [o_spec(k) for k in range(TOP_K)],
        out_specs=pl.BlockSpec((tm, d), lambda i: (i, 0)),
        out_shape=jax.ShapeDtypeStruct((n, d), F32),
        compiler_params=_cparams(("parallel",)),
        name="combine",
    )(x1, gate_col, o_rows, o_rows, o_rows, o_rows)


def _prep_weights(p):
    cw = p["w_dw"].shape[-1]
    o_gate = 2 * cw + 3 * MOBA_WIDTH + MEM_WIDTH
    row = lambda a: a.reshape(1, -1).astype(F32)
    tile = lambda g, n: jnp.tile(g.astype(F32), n).reshape(1, -1)
    prep = {}
    prep["proj"] = (
        row(p["g_norm1"]), p["w_in"][:, :o_gate].astype(BF16),
        _segment_ones(MOBA_WIDTH, MOBA_HEAD_DIM), _segment_ones(MEM_WIDTH, MEM_HEAD_DIM),
        tile(p["g_q_moba"], MOBA_HEADS), tile(p["g_k_moba"], MOBA_HEADS), tile(p["g_q_mem"], MEM_HEADS),
        p["w_dw"].astype(F32), row(p["b_dw"]), row(p["g_conv_ln"]), row(p["b_conv_ln"]),
    )
    prep["mem_kv"] = (row(p["g_mem_norm"]), p["w_mem_kv"].astype(BF16), _segment_ones(MEM_WIDTH, MEM_HEAD_DIM),
                      tile(p["g_k_mem"], MEM_HEADS))
    wr_t = p["w_router"].astype(F32).T
    wr_hi = wr_t.astype(BF16)
    wr_lo = (wr_t - wr_hi.astype(F32)).astype(BF16)
    prep["merge"] = (
        row(p["g_norm1"]), p["w_in"][:, o_gate:].astype(BF16), p["w_conv_out"].astype(BF16), row(p["b_conv_out"]),
        p["w_moba_o"].astype(BF16), p["w_mem_o"].astype(BF16), p["w_out"].astype(BF16), row(p["g_norm2"]),
        wr_hi, wr_lo, p["b_router"].astype(F32).reshape(-1, 1),
    )
    prep["moe"] = (p["w_gu"].astype(F32), p["b_gu"].astype(F32)[:, None, :], p["w_down"].astype(F32),
                   p["b_down"].astype(F32)[:, None, :])
    return prep


def _ffn(x2d, c, om, omem, tm, prep):
    n = x2d.shape[0]
    x1, h_rows, top_e, gate_col = _merge(x2d, c, om, omem, tm, prep["merge"])
    block_e, meta, n_rows = _route(top_e, n)
    o_rows = _moe(block_e, meta, h_rows, n_rows, *prep["moe"])
    return _combine(x1, gate_col, o_rows, tm)


_PARAM_NAMES = ("g_norm1", "w_in", "w_dw", "b_dw", "g_conv_ln", "b_conv_ln", "w_conv_out", "b_conv_out", "g_q_moba",
                "g_k_moba", "w_moba_o", "g_mem_norm", "w_mem_kv", "g_q_mem", "g_k_mem", "w_mem_o", "w_out", "g_norm2",
                "w_router", "b_router", "w_gu", "b_gu", "w_down", "b_down")

PROMPT_ROWS = 512
SAMPLE_BATCH_TILE = 64
SAMPLE_MEM_TILE = 8


def kernel(x_prompt, x_sample, mem_prompt, cache_moba_k, cache_moba_v, cache_mem_k, cache_mem_v, state_conv, page_table, g_norm1, w_in, w_dw, b_dw, g_conv_ln, b_conv_ln, w_conv_out, b_conv_out, g_q_moba, g_k_moba, w_moba_o, g_mem_norm, w_mem_kv, g_q_mem, g_k_mem, w_mem_o, w_out, g_norm2, w_router, b_router, w_gu, b_gu, w_down, b_down):
    params = dict(zip(_PARAM_NAMES, (g_norm1, w_in, w_dw, b_dw, g_conv_ln, b_conv_ln, w_conv_out, b_conv_out,
                                     g_q_moba, g_k_moba, w_moba_o, g_mem_norm, w_mem_kv, g_q_mem, g_k_mem, w_mem_o,
                                     w_out, g_norm2, w_router, b_router, w_gu, b_gu, w_down, b_down)))
    depth = g_norm1.shape[0]
    bp, lp, d = x_prompt.shape
    bs, ls, _ = x_sample.shape
    n_mem = mem_prompt.shape[1]
    y_p, y_s = x_prompt, x_sample
    outs = [[] for _ in range(8)]
    for layer in range(depth):
        prep = _prep_weights({k: v[layer] for k, v in params.items()})
        k, v, q, kb, vb, qm, c, tail = _proj(y_p, None, 1, PROMPT_ROWS, *prep["proj"])
        mk, mv = _mem_kv(mem_prompt.reshape(bp * n_mem, d), PROMPT_ROWS, *prep["mem_kv"])
        o_moba = _moba_prompt(q, kb, vb)
        o_mem = _mem_attend(qm, mk.reshape(1, bp, n_mem, MEM_WIDTH), mv.reshape(1, bp, n_mem, MEM_WIDTH), 0, 1,
                            PROMPT_ROWS)
        flat = lambda a: a.reshape(bp * lp, a.shape[-1])
        y_p = _ffn(flat(y_p), flat(c), flat(o_moba), flat(o_mem), PROMPT_ROWS, prep).reshape(bp, lp, d)
        mem_shape = (bp, n_mem, MEM_HEADS, MEM_HEAD_DIM)
        for lst, val in zip(outs[:5], (k, v, mk.reshape(mem_shape), mv.reshape(mem_shape), tail)):
            lst.append(val)
        k, v, q, kb, vb, qm, c, tail = _proj(y_s, state_conv[layer], SAMPLE_BATCH_TILE, ls, *prep["proj"])
        o_moba = _moba_sample(q, kb, vb, cache_moba_k, cache_moba_v, page_table, layer)
        o_mem = _mem_attend(qm, cache_mem_k, cache_mem_v, layer, SAMPLE_MEM_TILE, ls)
        flat = lambda a: a.reshape(bs * ls, a.shape[-1])
        y_s = _ffn(flat(y_s), flat(c), flat(o_moba), flat(o_mem), PROMPT_ROWS, prep).reshape(bs, ls, d)
        for lst, val in zip(outs[5:], (k, v, tail)):
            lst.append(val)
    kp, vp, mkp, mvp, cp, ks, vs, cs = (jnp.stack(o) for o in outs)
    return (y_p, y_s, kp, vp, mkp, mvp, cp, ks, vs, cs)
```

```python
import functools

import jax
import jax.numpy as jnp
from jax import lax
from jax.experimental import pallas as pl
from jax.experimental.pallas import tpu as pltpu

F32 = jnp.float32
BF16 = jnp.bfloat16

EPS = 1e-6
MOBA_HEADS = 8
MOBA_HEAD_DIM = 64
MOBA_WIDTH = MOBA_HEADS * MOBA_HEAD_DIM
MOBA_BLOCK = 256
MOBA_TOPK = 3
MEM_HEADS = 4
MEM_HEAD_DIM = 128
MEM_WIDTH = MEM_HEADS * MEM_HEAD_DIM
CONV_KERNEL = 31
CONV_HIST = CONV_KERNEL - 1
N_EXPERTS = 32
TOP_K = 4
SWIGLU_LIMIT = 7.0
SWIGLU_ALPHA = 1.702
MOE_BLOCK = 512

V7X_LANES = 128
V7X_SUBLANES = 8
HIST_PAD = 32
VMEM_LIMIT = 56 * 1024 * 1024


def _cparams(sem):
    return pltpu.CompilerParams(dimension_semantics=sem, vmem_limit_bytes=VMEM_LIMIT)


def _full(shape):
    n = len(shape)
    return pl.BlockSpec(shape, lambda *_: (0,) * n)


def _segment_ones(width, seg):
    r = jnp.arange(width) // seg
    return (r[:, None] == r[None, :]).astype(BF16)


def _rms(x, g):
    return x * lax.rsqrt(jnp.mean(x * x, axis=-1, keepdims=True) + EPS) * g


def _seg_rms(z, seg_ones, seg, g):
    sq = (z * z).astype(BF16)
    parts = []
    for c in range(0, z.shape[-1], 2 * V7X_LANES):
        parts.append(jnp.dot(sq[:, c:c + 2 * V7X_LANES], seg_ones[c:c + 2 * V7X_LANES, c:c + 2 * V7X_LANES],
                             preferred_element_type=F32))
    ms = jnp.concatenate(parts, axis=-1) * (1.0 / seg)
    return z * lax.rsqrt(ms + EPS) * g


def _proj_kernel(has_hist, carry, x_ref, *refs):
    if has_hist:
        hist_ref, refs = refs[0], refs[1:]
    (g1_ref, w1_ref, seg64_ref, seg128_ref, gq_ref, gk_ref, gqm_ref, wdw_ref, bdw_ref, gln_ref, bln_ref,
     k_out, v_out, q_out, kb_out, vb_out, qm_out, c_out, tail_out, xp_scr) = refs
    bt, tl, d = x_ref.shape
    rows = bt * tl
    cw = c_out.shape[-1]
    t = pl.program_id(1)

    x = x_ref[...].reshape(rows, d)
    h = _rms(x, g1_ref[...]).astype(BF16)
    z = jnp.dot(h, w1_ref[...], preferred_element_type=F32)
    o_q = 2 * cw
    o_k = o_q + MOBA_WIDTH
    o_v = o_k + MOBA_WIDTH
    o_qm = o_v + MOBA_WIDTH
    u = z[:, :cw] * jax.nn.sigmoid(z[:, cw:o_q])
    seg64 = seg64_ref[...]
    qn = _seg_rms(z[:, o_q:o_k], seg64, MOBA_HEAD_DIM, gq_ref[...])
    kn = _seg_rms(z[:, o_k:o_v], seg64, MOBA_HEAD_DIM, gk_ref[...])
    vv = z[:, o_v:o_qm]
    qmn = _seg_rms(z[:, o_qm:o_qm + MEM_WIDTH], seg128_ref[...], MEM_HEAD_DIM, gqm_ref[...])

    q_out[...] = qn.astype(BF16).reshape(bt, tl, MOBA_WIDTH)
    kb_out[...] = kn.astype(BF16).reshape(bt, tl, MOBA_WIDTH)
    vb_out[...] = vv.astype(BF16).reshape(bt, tl, MOBA_WIDTH)
    qm_out[...] = qmn.astype(BF16).reshape(bt, tl, MEM_WIDTH)
    kn3 = kn.reshape(bt, tl, MOBA_WIDTH)
    vv3 = vv.reshape(bt, tl, MOBA_WIDTH)
    for hh in range(MOBA_HEADS):
        sl = slice(hh * MOBA_HEAD_DIM, (hh + 1) * MOBA_HEAD_DIM)
        k_out[:, hh, :, :] = kn3[:, :, sl]
        v_out[:, hh, :, :] = vv3[:, :, sl]

    lo = HIST_PAD - CONV_HIST

    @pl.when(t == 0)
    def _():
        xp_scr[:, 0:HIST_PAD, :] = jnp.zeros((bt, HIST_PAD, cw), F32)
        if has_hist:
            xp_scr[:, lo:HIST_PAD, :] = hist_ref[...]

    xp_scr[:, HIST_PAD:HIST_PAD + tl, :] = u.reshape(bt, tl, cw)
    rc = min(tl, 64)
    bdw = bdw_ref[...]
    gln = gln_ref[...]
    bln = bln_ref[...]
    for r0 in range(0, tl, rc):
        window = xp_scr[:, r0:r0 + rc + HIST_PAD, :]
        acc = jnp.zeros((bt, rc, cw), F32) + bdw
        for s in range(V7X_SUBLANES):
            ext = rc if s == 0 else rc + V7X_SUBLANES
            part = None
            for q in range(HIST_PAD // V7X_SUBLANES + 1):
                j = V7X_SUBLANES * q + s - lo
                if 0 <= j < CONV_KERNEL:
                    term = wdw_ref[j:j + 1, :] * window[:, V7X_SUBLANES * q:V7X_SUBLANES * q + ext, :]
                    part = term if part is None else part + term
            acc = acc + (part if s == 0 else part[:, s:s + rc, :])
        mu = jnp.mean(acc, axis=-1, keepdims=True)
        dev = acc - mu
        var = jnp.mean(dev * dev, axis=-1, keepdims=True)
        y = dev * lax.rsqrt(var + EPS) * gln + bln
        c_out[:, r0:r0 + rc, :] = (y * jax.nn.sigmoid(y)).astype(BF16)
    tail_out[...] = xp_scr[:, tl + lo:tl + HIST_PAD, :]
    if carry:
        xp_scr[:, 0:HIST_PAD, :] = xp_scr[:, tl:tl + HIST_PAD, :]


def _proj(x, hist, bt, tl, g1, w1, seg64, seg128, gq, gk, gqm, wdw, bdw, gln, bln):
    b, l, d = x.shape
    cw = wdw.shape[-1]
    n_t = l // tl
    has_hist = hist is not None
    carry = n_t > 1
    assert b % bt == 0 and l % tl == 0 and (not carry or (tl >= HIST_PAD and bt == 1))
    tok = lambda w: pl.BlockSpec((bt, tl, w), lambda i, j: (i, j, 0))
    head = pl.BlockSpec((bt, MOBA_HEADS, tl, MOBA_HEAD_DIM), lambda i, j: (i, 0, j, 0))
    in_specs = [tok(d)]
    args = [x]
    if has_hist:
        in_specs.append(pl.BlockSpec((bt, CONV_HIST, cw), lambda i, j: (i, 0, 0)))
        args.append(hist)
    consts = [g1, w1, seg64, seg128, gq, gk, gqm, wdw, bdw, gln, bln]
    in_specs += [_full(c.shape) for c in consts]
    out_shape = [
        jax.ShapeDtypeStruct((b, MOBA_HEADS, l, MOBA_HEAD_DIM), F32),
        jax.ShapeDtypeStruct((b, MOBA_HEADS, l, MOBA_HEAD_DIM), F32),
        jax.ShapeDtypeStruct((b, l, MOBA_WIDTH), BF16),
        jax.ShapeDtypeStruct((b, l, MOBA_WIDTH), BF16),
        jax.ShapeDtypeStruct((b, l, MOBA_WIDTH), BF16),
        jax.ShapeDtypeStruct((b, l, MEM_WIDTH), BF16),
        jax.ShapeDtypeStruct((b, l, cw), BF16),
        jax.ShapeDtypeStruct((b, CONV_HIST, cw), F32),
    ]
    out_specs = [head, head, tok(MOBA_WIDTH), tok(MOBA_WIDTH), tok(MOBA_WIDTH), tok(MEM_WIDTH), tok(cw),
                 pl.BlockSpec((bt, CONV_HIST, cw), lambda i, j: (i, 0, 0))]
    return pl.pallas_call(
        functools.partial(_proj_kernel, has_hist, carry),
        grid=(b // bt, n_t),
        in_specs=in_specs,
        out_specs=out_specs,
        out_shape=out_shape,
        scratch_shapes=[pltpu.VMEM((bt, HIST_PAD + tl, cw), F32)],
        compiler_params=_cparams(("parallel", "arbitrary")),
        name="proj",
    )(*args, *consts)


def _memkv_kernel(x_ref, g_ref, w_ref, seg128_ref, gk_ref, k_out, v_out):
    h = _rms(x_ref[...], g_ref[...]).astype(BF16)
    kv = jnp.dot(h, w_ref[...], preferred_element_type=F32)
    k_out[...] = _seg_rms(kv[:, :MEM_WIDTH], seg128_ref[...], MEM_HEAD_DIM, gk_ref[...])
    v_out[...] = kv[:, MEM_WIDTH:]


def _mem_kv(mem2d, tm, g, w, seg128, gk):
    n, d = mem2d.shape
    assert n % tm == 0
    spec = pl.BlockSpec((tm, MEM_WIDTH), lambda i: (i, 0))
    return pl.pallas_call(
        _memkv_kernel,
        grid=(n // tm,),
        in_specs=[pl.BlockSpec((tm, d), lambda i: (i, 0))] + [_full(c.shape) for c in (g, w, seg128, gk)],
        out_specs=[spec, spec],
        out_shape=[jax.ShapeDtypeStruct((n, MEM_WIDTH), F32)] * 2,
        compiler_params=_cparams(("parallel",)),
        name="mem_kv",
    )(mem2d, g, w, seg128, gk)


def _mematt_kernel(q_ref, k_ref, v_ref, o_ref):
    scale = MEM_HEAD_DIM ** -0.5
    for hh in range(MEM_HEADS):
        sl = slice(hh * MEM_HEAD_DIM, (hh + 1) * MEM_HEAD_DIM)
        q = q_ref[:, :, sl]
        if len(k_ref.shape) == 5:
            k = k_ref[0, :, :, hh, :].astype(BF16)
            v = v_ref[0, :, :, hh, :].astype(BF16)
        else:
            k = k_ref[0, :, :, sl].astype(BF16)
            v = v_ref[0, :, :, sl].astype(BF16)
        s = jnp.einsum("bqd,bkd->bqk", q, k, preferred_element_type=F32) * scale
        m = jnp.max(s, axis=-1, keepdims=True)
        p = jnp.exp(s - m)
        l = jnp.sum(p, axis=-1, keepdims=True)
        o = jnp.einsum("bqk,bkd->bqd", p.astype(BF16), v, preferred_element_type=F32)
        o_ref[:, :, sl] = (o / l).astype(BF16)


def _mem_attend(qm, mk, mv, layer, bt, tq):
    b, l, w = qm.shape
    m = mk.shape[2]
    assert b % bt == 0 and l % tq == 0 and mk.shape[3:] in ((MEM_HEADS, MEM_HEAD_DIM), (MEM_WIDTH,))
    tail = mk.shape[3:]
    kv_spec = pl.BlockSpec((1, bt, m) + tail, lambda i, j: (layer, i, 0) + (0,) * len(tail))
    return pl.pallas_call(
        _mematt_kernel,
        grid=(b // bt, l // tq),
        in_specs=[pl.BlockSpec((bt, tq, w), lambda i, j: (i, j, 0)), kv_spec, kv_spec],
        out_specs=pl.BlockSpec((bt, tq, w), lambda i, j: (i, j, 0)),
        out_shape=jax.ShapeDtypeStruct((b, l, w), BF16),
        compiler_params=_cparams(("parallel", "arbitrary")),
        name="mem_attend",
    )(qm, mk, mv)


def _moba_prompt_kernel(q_ref, k_ref, v_ref, o_ref, vt_scr, km_scr, sel_scr, acc_scr, m_scr, l_scr, wq_scr):
    blk = MOBA_BLOCK
    l = k_ref.shape[1]
    nb = l // blk
    qb = pl.program_id(1)
    hd = MOBA_HEAD_DIM

    @pl.when(qb == 0)
    def _():
        lane_head = lax.broadcasted_iota(jnp.int32, (MOBA_HEADS, MOBA_WIDTH), 1) // hd
        row_head = lax.broadcasted_iota(jnp.int32, (MOBA_HEADS, MOBA_WIDTH), 0)
        head_mask = (lane_head == row_head).astype(F32)
        for n in range(nb):
            vt_scr[n] = v_ref[0, n * blk:(n + 1) * blk, :].T
            kmean = jnp.mean(k_ref[0, n * blk:(n + 1) * blk, :].astype(F32), axis=0, keepdims=True)
            km_scr[n * MOBA_HEADS:(n + 1) * MOBA_HEADS, :] = (kmean * head_mask).astype(BF16)

    qt = q_ref[0].T
    gate = jnp.dot(km_scr[...], qt, preferred_element_type=F32).reshape(nb, MOBA_HEADS, blk)
    for n in range(nb):
        rank = jnp.zeros((MOBA_HEADS, blk), jnp.int32)
        for n2 in range(nb):
            if n2 == n:
                continue
            beats = (gate[n2] >= gate[n]) if n2 < n else (gate[n2] > gate[n])
            rank = rank + jnp.where(jnp.logical_and(beats, n2 < qb), 1, 0)
        sel_scr[n] = jnp.where(jnp.logical_and(rank < MOBA_TOPK, n < qb), 1.0, 0.0)

    pairs = MOBA_HEADS // 2
    row_in_pair = lax.broadcasted_iota(jnp.int32, (2 * hd, blk), 0) // hd
    for pr in range(pairs):
        qp = qt[pr * 2 * hd:(pr + 1) * 2 * hd, :] * jnp.asarray(hd ** -0.5, BF16)
        zero = jnp.zeros_like(qp)
        wq_scr[pr] = jnp.concatenate([jnp.where(row_in_pair == 0, qp, zero), jnp.where(row_in_pair == 1, qp, zero)],
                                     axis=1)

    def attend(kblk, n, keep, first):
        s_pairs = [jnp.dot(kblk[:, pr * 2 * hd:(pr + 1) * 2 * hd], wq_scr[pr], preferred_element_type=F32)
                   for pr in range(pairs)]
        m_old = None if first else m_scr[...]
        l_old = None if first else l_scr[...]
        m_rows, l_rows, alphas, probs = [], [], [], []
        for hh in range(MOBA_HEADS):
            s = jnp.where(keep(hh), s_pairs[hh // 2][:, (hh % 2) * blk:(hh % 2 + 1) * blk], -jnp.inf)
            m_new = jnp.max(s, axis=0, keepdims=True)
            if not first:
                m_new = jnp.maximum(m_old[hh:hh + 1, :], m_new)
                alphas.append(jnp.exp(m_old[hh:hh + 1, :] - m_new))
            p = jnp.exp(s - m_new)
            p_sum = jnp.sum(p, axis=0, keepdims=True)
            l_rows.append(p_sum if first else alphas[hh] * l_old[hh:hh + 1, :] + p_sum)
            m_rows.append(m_new)
            probs.append(p.astype(BF16))
        m_scr[...] = jnp.concatenate(m_rows, axis=0)
        l_scr[...] = jnp.concatenate(l_rows, axis=0)
        pvs = [jnp.dot(vt_scr[n, hh * hd:(hh + 1) * hd, :], probs[hh], preferred_element_type=F32)
               for hh in range(MOBA_HEADS)]
        for hh in range(MOBA_HEADS):
            rows = slice(hh * hd, (hh + 1) * hd)
            acc_scr[rows, :] = pvs[hh] if first else alphas[hh] * acc_scr[rows, :] + pvs[hh]

    causal = lax.broadcasted_iota(jnp.int32, (blk, blk), 0) <= lax.broadcasted_iota(jnp.int32, (blk, blk), 1)
    attend(k_ref[0, pl.ds(pl.multiple_of(qb * blk, blk), blk), :], qb, lambda hh: causal, True)

    def past_block(n, carry):
        sel = sel_scr[n]
        attend(k_ref[0, pl.ds(pl.multiple_of(n * blk, blk), blk), :], n, lambda hh: sel[hh:hh + 1, :] > 0.0, False)
        return carry

    lax.fori_loop(0, qb, past_block, 0)

    for hh in range(MOBA_HEADS):
        acc_scr[hh * hd:(hh + 1) * hd, :] = acc_scr[hh * hd:(hh + 1) * hd, :] / l_scr[hh:hh + 1, :]
    o_ref[0] = acc_scr[...].T.astype(BF16)


def _moba_prompt(q, kb, vb):
    b, l, w = q.shape
    blk = MOBA_BLOCK
    assert l % blk == 0
    nb = l // blk
    full = pl.BlockSpec((1, l, w), lambda i, j: (i, 0, 0))
    tile = pl.BlockSpec((1, blk, w), lambda i, j: (i, j, 0))
    return pl.pallas_call(
        _moba_prompt_kernel,
        grid=(b, nb),
        in_specs=[tile, full, full],
        out_specs=tile,
        out_shape=jax.ShapeDtypeStruct((b, l, w), BF16),
        scratch_shapes=[pltpu.VMEM((nb, w, blk), BF16), pltpu.VMEM((nb * MOBA_HEADS, w), BF16),
                        pltpu.VMEM((nb, MOBA_HEADS, blk), F32), pltpu.VMEM((w, blk), F32),
                        pltpu.VMEM((MOBA_HEADS, blk), F32), pltpu.VMEM((MOBA_HEADS, blk), F32),
                        pltpu.VMEM((MOBA_HEADS // 2, 2 * MOBA_HEAD_DIM, 2 * blk), BF16)],
        compiler_params=_cparams(("parallel", "arbitrary")),
        name="moba_prompt",
    )(q, kb, vb)


SAMPLE_GROUP = 4
SAMPLE_RING = 3


def _moba_sample_kernel(layer, pt_ref, q_ref, kn_ref, vn_ref, ck_hbm, cv_hbm, o_ref,
                        kbuf, vbuf, ksem, vsem, m_scr, l_scr, g_scr, o_scr):
    b = pl.program_id(0)
    nb = pl.num_programs(0)
    n_pages = pt_ref.shape[1]
    page = kbuf.shape[4]
    ppb = MOBA_BLOCK // page
    n_full = n_pages // ppb
    grp = SAMPLE_GROUP
    n_grp = n_full // grp
    hd = MOBA_HEAD_DIM
    t = q_ref.shape[1]
    ncol = MOBA_HEADS * t
    scale = hd ** -0.5
    nt = (((1,), (1,)), ((), ()))

    def copies(bb, g, par):
        out = []
        for j in range(grp):
            for pg in range(ppb):
                pid = pt_ref[bb, (g * grp + j) * ppb + pg]
                out.append(pltpu.make_async_copy(ck_hbm.at[layer, pid], kbuf.at[par, j * ppb + pg], ksem.at[par]))
                out.append(pltpu.make_async_copy(cv_hbm.at[layer, pid], vbuf.at[par, j * ppb + pg], vsem.at[par]))
        return out

    ahead = SAMPLE_RING - 1

    @pl.when(b == 0)
    def _():
        for f in range(min(ahead, pt_ref.shape[0] * n_grp)):
            for c in copies(f // n_grp, f % n_grp, f % SAMPLE_RING):
                c.start()

    q = q_ref[0]
    row_head = lax.broadcasted_iota(jnp.int32, (V7X_LANES, MOBA_WIDTH), 0) // t
    lane_head = lax.broadcasted_iota(jnp.int32, (V7X_LANES, MOBA_WIDTH), 1) // hd
    q_tiled = jnp.concatenate([q.astype(F32)] * (V7X_LANES // t), axis=0)
    qrows = jnp.where(row_head == lane_head, q_tiled, 0.0).astype(BF16)

    def packed(buf, par, j):
        return jnp.concatenate([buf[par, j * ppb + pg].reshape(MOBA_WIDTH, page) for pg in range(ppb)],
                               axis=-1).astype(BF16)

    lane_id = lax.broadcasted_iota(jnp.int32, (V7X_LANES, V7X_LANES), 1)
    g_scr[...] = jnp.full((V7X_LANES, V7X_LANES), -jnp.inf, F32)
    m_scr[...] = jnp.full((V7X_LANES, V7X_LANES), -jnp.inf, F32)
    l_scr[...] = jnp.zeros((V7X_LANES, V7X_LANES), F32)

    def group(g, carry):
        flat = b * n_grp + g
        par = flat % SAMPLE_RING
        nxt = flat + ahead

        @pl.when(nxt < nb * n_grp)
        def _():
            for c in copies(nxt // n_grp, nxt % n_grp, nxt % SAMPLE_RING):
                c.start()

        for c in copies(b, g, par):
            c.wait()
        for j in range(grp):
            n = g * grp + j
            s_raw = jnp.dot(qrows, packed(kbuf, par, j), preferred_element_type=F32)
            s = s_raw * scale
            m = jnp.max(s, axis=-1, keepdims=True)
            p = jnp.exp(s - m)
            hit = lane_id == n
            g_scr[...] = jnp.where(hit, jnp.sum(s_raw, axis=-1, keepdims=True) * (1.0 / MOBA_BLOCK), g_scr[...])
            m_scr[...] = jnp.where(hit, m, m_scr[...])
            l_scr[...] = jnp.where(hit, jnp.sum(p, axis=-1, keepdims=True), l_scr[...])
            o_scr[n] = lax.dot_general(p[:ncol, :].astype(BF16), packed(vbuf, par, j), nt, preferred_element_type=F32)
        return carry

    lax.fori_loop(0, n_grp, group, 0)

    gate = g_scr[...]
    sel = jnp.zeros(gate.shape, jnp.bool_)
    for _ in range(min(MOBA_TOPK, n_full)):
        mx = jnp.max(gate, axis=-1, keepdims=True)
        first = jnp.min(jnp.where(gate == mx, lane_id, V7X_LANES), axis=-1, keepdims=True)
        hit = lane_id == first
        sel = jnp.logical_or(sel, hit)
        gate = jnp.where(hit, -jnp.inf, gate)

    s_new = lax.dot_general(qrows, kn_ref[0], nt, preferred_element_type=F32) * scale
    row_tok = lax.broadcasted_iota(jnp.int32, s_new.shape, 0) % t
    key_tok = lax.broadcasted_iota(jnp.int32, s_new.shape, 1)
    s_new = jnp.where(key_tok <= row_tok, s_new, -jnp.inf)
    m_blk = jnp.where(sel, m_scr[...], -jnp.inf)
    m_fin = jnp.maximum(jnp.max(m_blk, axis=-1, keepdims=True), jnp.max(s_new, axis=-1, keepdims=True))
    w_blk = jnp.where(sel, jnp.exp(m_blk - m_fin), 0.0)
    p_new = jnp.exp(s_new - m_fin)
    l_fin = jnp.sum(w_blk * l_scr[...], axis=-1, keepdims=True) + jnp.sum(p_new, axis=-1, keepdims=True)
    o_fin = jnp.dot(p_new[:ncol, :].astype(BF16), vn_ref[0], preferred_element_type=F32)
    for n in range(n_full):
        o_fin = o_fin + w_blk[:ncol, n:n + 1] * o_scr[n]
    o_fin = o_fin / l_fin[:ncol, :]
    for hh in range(MOBA_HEADS):
        o_ref[0, :, hh * hd:(hh + 1) * hd] = o_fin[hh * t:(hh + 1) * t, hh * hd:(hh + 1) * hd].astype(BF16)


def _moba_sample(q, kn, vn, cache_k, cache_v, page_table, layer):
    b, t, w = q.shape
    n_pages = page_table.shape[1]
    _, _, heads, page, hd = cache_k.shape
    cache_k, cache_v = jnp.swapaxes(cache_k, 3, 4), jnp.swapaxes(cache_v, 3, 4)
    ppb = MOBA_BLOCK // page
    assert MOBA_BLOCK % page == 0 and n_pages % ppb == 0 and heads == MOBA_HEADS and hd == MOBA_HEAD_DIM
    n_full = n_pages // ppb
    assert n_full % SAMPLE_GROUP == 0 and MOBA_TOPK <= n_full <= V7X_LANES and heads * t <= V7X_LANES
    tok = pl.BlockSpec((1, t, w), lambda i, pt: (i, 0, 0))
    any_spec = pl.BlockSpec(memory_space=pl.ANY)
    pages = SAMPLE_GROUP * ppb
    grid_spec = pltpu.PrefetchScalarGridSpec(
        num_scalar_prefetch=1,
        grid=(b,),
        in_specs=[tok, tok, tok, any_spec, any_spec],
        out_specs=tok,
        scratch_shapes=[pltpu.VMEM((SAMPLE_RING, pages, heads, hd, page), F32),
                        pltpu.VMEM((SAMPLE_RING, pages, heads, hd, page), F32),
                        pltpu.SemaphoreType.DMA((SAMPLE_RING,)), pltpu.SemaphoreType.DMA((SAMPLE_RING,)),
                        pltpu.VMEM((V7X_LANES, V7X_LANES), F32), pltpu.VMEM((V7X_LANES, V7X_LANES), F32),
                        pltpu.VMEM((V7X_LANES, V7X_LANES), F32), pltpu.VMEM((n_full, heads * t, w), F32)],
    )
    return pl.pallas_call(
        functools.partial(_moba_sample_kernel, layer),
        grid_spec=grid_spec,
        out_shape=jax.ShapeDtypeStruct((b, t, w), BF16),
        compiler_params=_cparams(("arbitrary",)),
        name="moba_sample",
    )(page_table, q, kn, vn, cache_k, cache_v)


def _merge_kernel(x_ref, c_ref, om_ref, omem_ref, g1_ref, wg_ref, wco_ref, bco_ref, wmo_ref, wmemo_ref, wout_ref,
                  g2_ref, wr_hi_ref, wr_lo_ref, br_ref, x1_out, h2_out, e_out, gate_out):
    tm, d = x_ref.shape
    x = x_ref[...]
    h = _rms(x, g1_ref[...]).astype(BF16)
    gates = jax.nn.sigmoid(jnp.dot(h, wg_ref[...], preferred_element_type=F32))
    u_conv = jnp.dot(c_ref[...], wco_ref[...], preferred_element_type=F32) + bco_ref[...]
    u_moba = jnp.dot(om_ref[...], wmo_ref[...], preferred_element_type=F32)
    u_mem = jnp.dot(omem_ref[...], wmemo_ref[...], preferred_element_type=F32)
    merged = gates[:, :d] * u_conv + gates[:, d:2 * d] * u_moba + gates[:, 2 * d:] * u_mem
    x1 = x + jnp.dot(merged.astype(BF16), wout_ref[...], preferred_element_type=F32)
    x1_out[...] = x1
    h2 = _rms(x1, g2_ref[...])
    for c in range(d // V7X_LANES):
        h2_out[pl.ds(c, tm, stride=d // V7X_LANES), :] = h2[:, c * V7X_LANES:(c + 1) * V7X_LANES]

    h2_hi = h2.astype(BF16)
    h2_lo = (h2 - h2_hi.astype(F32)).astype(BF16)
    nt = (((1,), (1,)), ((), ()))
    logits = (lax.dot_general(wr_hi_ref[...], h2_hi, nt, preferred_element_type=F32)
              + lax.dot_general(wr_hi_ref[...], h2_lo, nt, preferred_element_type=F32)
              + lax.dot_general(wr_lo_ref[...], h2_hi, nt, preferred_element_type=F32)) + br_ref[...]
    e_id = lax.broadcasted_iota(jnp.int32, logits.shape, 0)
    tops, ids = [], []
    for _ in range(TOP_K):
        mx = jnp.max(logits, axis=0, keepdims=True)
        first = jnp.min(jnp.where(logits == mx, e_id, N_EXPERTS), axis=0, keepdims=True)
        tops.append(mx)
        ids.append(first)
        logits = jnp.where(e_id == first, -jnp.inf, logits)
    ex = [jnp.exp(v - tops[0]) for v in tops]
    denom = ex[0] + ex[1] + ex[2] + ex[3]
    e_out[...] = jnp.concatenate(ids, axis=0)
    gate_rows = jnp.concatenate([v / denom for v in ex] + [jnp.zeros((V7X_LANES - TOP_K, tm), F32)], axis=0)
    gate_out[...] = gate_rows.T


def _merge(x2d, c, om, omem, tm, consts):
    n, d = x2d.shape
    assert n % tm == 0
    row = lambda w: pl.BlockSpec((tm, w), lambda i: (i, 0))
    n_chunk = d // V7X_LANES
    return pl.pallas_call(
        _merge_kernel,
        grid=(n // tm,),
        in_specs=[row(d), row(c.shape[1]), row(om.shape[1]), row(omem.shape[1])] + [_full(a.shape) for a in consts],
        out_specs=[row(d), pl.BlockSpec((tm * n_chunk, V7X_LANES), lambda i: (i, 0)),
                   pl.BlockSpec((TOP_K, tm), lambda i: (0, i)), row(V7X_LANES)],
        out_shape=[jax.ShapeDtypeStruct((n, d), F32), jax.ShapeDtypeStruct((n * n_chunk, V7X_LANES), F32),
                   jax.ShapeDtypeStruct((TOP_K, n), jnp.int32), jax.ShapeDtypeStruct((n, V7X_LANES), F32)],
        compiler_params=_cparams(("parallel",)),
        name="merge",
    )(x2d, c, om, omem, *consts)


META_ROWS = V7X_SUBLANES
TOK_ROW0 = 0
DST_ROW0 = MOE_BLOCK // V7X_LANES


def _moe_kernel(first_spill, be_ref, meta_hbm, h_hbm, wgu_ref, bgu_ref, wd_ref, bd_ref, o_hbm,
                meta_smem, msem, xbuf, xsem, ybuf, ysem, wgu_bf, wd_bf):
    i = pl.program_id(0)
    n = be_ref[pl.num_programs(0)]
    rows = MOE_BLOCK
    n_chunk = xbuf.shape[1] // rows

    def meta_copy(blk):
        return pltpu.make_async_copy(meta_hbm.at[pl.ds(pl.multiple_of(blk * META_ROWS, META_ROWS), META_ROWS)],
                                     meta_smem.at[blk % 3], msem.at[blk % 3])

    def gather_start(blk, slot):
        ms = blk % 3
        for r in range(rows):
            src = pl.multiple_of(meta_smem[ms, TOK_ROW0 + r // V7X_LANES, r % V7X_LANES], n_chunk)
            pltpu.make_async_copy(h_hbm.at[pl.ds(src, n_chunk)], xbuf.at[slot, pl.ds(r * n_chunk, n_chunk)],
                                  xsem.at[slot]).start(priority=r % 2)

    def scatter_start(blk, slot, r0, r1):
        ms = blk % 3
        for r in range(r0, r1):
            dst = pl.multiple_of(meta_smem[ms, DST_ROW0 + r // V7X_LANES, r % V7X_LANES], n_chunk)
            pltpu.make_async_copy(ybuf.at[slot, pl.ds(r * n_chunk, n_chunk)], o_hbm.at[pl.ds(dst, n_chunk)],
                                  ysem.at[slot]).start(priority=r % 2)

    def gather_wait(slot):
        pltpu.make_async_copy(h_hbm.at[pl.ds(0, rows * n_chunk)], xbuf.at[slot], xsem.at[slot]).wait()

    def scatter_wait(slot):
        pltpu.make_async_copy(ybuf.at[slot], o_hbm.at[pl.ds(0, rows * n_chunk)], ysem.at[slot]).wait()

    slot = i % 2

    @pl.when(i < n)
    def _():
        @pl.when(i == 0)
        def _():
            meta_copy(0).start()
            ybuf[0] = jnp.zeros(ybuf.shape[1:], F32)
            n_spill = o_hbm.shape[0] - first_spill * n_chunk
            fills = [pltpu.make_async_copy(ybuf.at[0, pl.ds(0, min(rows * n_chunk, n_spill - s))],
                                           o_hbm.at[pl.ds(first_spill * n_chunk + s, min(rows * n_chunk, n_spill - s))],
                                           ysem.at[0])
                     for s in range(0, n_spill, rows * n_chunk)]
            for f in fills:
                f.start()
            for f in fills:
                f.wait()
            meta_copy(0).wait()
            gather_start(0, 0)

            @pl.when(n > 1)
            def _():
                meta_copy(1).start()

        @pl.when(i + 2 < n)
        def _():
            meta_copy(i + 2).start()

        @pl.when(i + 1 < n)
        def _():
            meta_copy(i + 1).wait()

        @pl.when(i >= 2)
        def _():
            scatter_wait(slot)

        @pl.when(jnp.logical_or(i == 0, be_ref[i] != be_ref[jnp.maximum(i - 1, 0)]))
        def _():
            wgu_bf[...] = wgu_ref[0].astype(BF16)
            wd_bf[...] = wd_ref[0].astype(BF16)

        gather_start(jnp.minimum(i + 1, n - 1), 1 - slot)
        gather_wait(slot)
        x = jnp.concatenate([xbuf[slot, pl.ds(c, rows, stride=n_chunk), :] for c in range(n_chunk)],
                            axis=-1).astype(BF16)
        gu = jnp.dot(x, wgu_bf[...], preferred_element_type=F32) + bgu_ref[0]
        dff = gu.shape[-1] // 2
        g = jnp.minimum(gu[:, :dff], SWIGLU_LIMIT)
        u = jnp.clip(gu[:, dff:], -SWIGLU_LIMIT, SWIGLU_LIMIT)
        act = (g * jax.nn.sigmoid(SWIGLU_ALPHA * g) * (u + 1.0)).astype(BF16)
        y = jnp.dot(act, wd_bf[...], preferred_element_type=F32) + bd_ref[0]
        for c in range(n_chunk):
            ybuf[slot, pl.ds(c, rows, stride=n_chunk), :] = y[:, c * V7X_LANES:(c + 1) * V7X_LANES]
        scatter_start(i, slot, 0, rows)

        @pl.when(i == n - 1)
        def _():
            gather_wait(1 - slot)
            scatter_wait(slot)

            @pl.when(n > 1)
            def _():
                scatter_wait(1 - slot)


def _moe(block_e, meta, h_rows, n_out_rows, wgu, bgu, wd, bd):
    n_blocks = block_e.shape[0] - 1
    d = wgu.shape[1]
    lanes = h_rows.shape[1]
    n_chunk = d // lanes
    any_spec = pl.BlockSpec(memory_space=pl.ANY)
    grid_spec = pltpu.PrefetchScalarGridSpec(
        num_scalar_prefetch=1,
        grid=(n_blocks,),
        in_specs=[any_spec, any_spec,
                  pl.BlockSpec((1, d, wgu.shape[2]), lambda i, be: (be[i], 0, 0)),
                  pl.BlockSpec((1, 1, bgu.shape[2]), lambda i, be: (be[i], 0, 0)),
                  pl.BlockSpec((1, wd.shape[1], d), lambda i, be: (be[i], 0, 0)),
                  pl.BlockSpec((1, 1, d), lambda i, be: (be[i], 0, 0))],
        out_specs=any_spec,
        scratch_shapes=[pltpu.SMEM((3, META_ROWS, V7X_LANES), jnp.int32), pltpu.SemaphoreType.DMA((3,)),
                        pltpu.VMEM((2, MOE_BLOCK * n_chunk, lanes), F32), pltpu.SemaphoreType.DMA((2,)),
                        pltpu.VMEM((2, MOE_BLOCK * n_chunk, lanes), F32), pltpu.SemaphoreType.DMA((2,)),
                        pltpu.VMEM(wgu.shape[1:], BF16), pltpu.VMEM(wd.shape[1:], BF16)],
    )
    first_spill = TOP_K * (h_rows.shape[0] // n_chunk)
    return pl.pallas_call(
        functools.partial(_moe_kernel, first_spill),
        grid_spec=grid_spec,
        out_shape=jax.ShapeDtypeStruct((n_out_rows * n_chunk, lanes), F32),
        compiler_params=_cparams(("arbitrary",)),
        name="moe",
    )(block_e, meta * n_chunk, h_rows, wgu, bgu, wd, bd)


def _route(top_e, n_tok):
    nk = n_tok * TOP_K
    flat_e = top_e.reshape(nk)
    counts = jnp.sum((flat_e[None, :] == jnp.arange(N_EXPERTS, dtype=jnp.int32)[:, None]).astype(jnp.int32), axis=1)
    padded = (counts + MOE_BLOCK - 1) // MOE_BLOCK * MOE_BLOCK
    pad_end = jnp.cumsum(padded)
    n_blocks = (nk + N_EXPERTS * (MOE_BLOCK - 1) + MOE_BLOCK - 1) // MOE_BLOCK
    n_rows = n_blocks * MOE_BLOCK
    shift = (nk + 1).bit_length()
    assert N_EXPERTS < (1 << (31 - shift))
    pad_cum = jnp.cumsum(padded - counts)
    pad_e = jnp.sum((pad_cum[None, :] <= jnp.arange(n_rows - nk, dtype=jnp.int32)[:, None]).astype(jnp.int32), axis=1)
    packed = jnp.concatenate([(flat_e << shift) + jnp.arange(1, nk + 1, dtype=jnp.int32), pad_e << shift])
    row_asg = (lax.sort(packed, is_stable=False) & ((1 << shift) - 1)) - 1
    is_real = row_asg >= 0
    row_tok = jnp.where(is_real, row_asg % n_tok, 0)
    spill = nk + jnp.cumsum(jnp.logical_not(is_real).astype(jnp.int32)) - 1
    row_dst = jnp.where(is_real, row_asg, spill)
    blk_start = jnp.arange(n_blocks, dtype=jnp.int32) * MOE_BLOCK
    block_e = jnp.sum((pad_end[None, :] <= blk_start[:, None]).astype(jnp.int32), axis=1)
    n_used = pad_end[N_EXPERTS - 1] // MOE_BLOCK
    last_e = jnp.max(jnp.where(counts > 0, jnp.arange(N_EXPERTS, dtype=jnp.int32), 0))
    block_e = jnp.concatenate([jnp.minimum(block_e, last_e), n_used[None]]).astype(jnp.int32)
    per = MOE_BLOCK // V7X_LANES
    meta = jnp.concatenate([row_tok.reshape(n_blocks, per, V7X_LANES), row_dst.reshape(n_blocks, per, V7X_LANES),
                            jnp.zeros((n_blocks, META_ROWS - 2 * per, V7X_LANES), jnp.int32)], axis=1)
    return block_e, meta.reshape(n_blocks * META_ROWS, V7X_LANES), n_rows


def _combine_kernel(x1_ref, gate_ref, o0_ref, o1_ref, o2_ref, o3_ref, y_ref):
    gate = gate_ref[...]
    tm, d = x1_ref.shape
    n_chunk = d // V7X_LANES
    cols = [gate[:, k:k + 1] for k in range(TOP_K)]
    for c in range(n_chunk):
        sl = slice(c * V7X_LANES, (c + 1) * V7X_LANES)
        acc = x1_ref[:, sl]
        for k, o_ref in enumerate((o0_ref, o1_ref, o2_ref, o3_ref)):
            acc = acc + cols[k] * o_ref[pl.ds(c, tm, stride=n_chunk), :]
        y_ref[:, sl] = acc


def _combine(x1, gate_col, o_rows, tm):
    n, d = x1.shape
    lanes = o_rows.shape[1]
    n_chunk = d // lanes
    assert n % tm == 0
    per = n // tm
    o_spec = lambda k: pl.BlockSpec((tm * n_chunk, lanes), lambda i, k=k: (k * per + i, 0))
    return pl.pallas_call(
        _combine_kernel,
        grid=(per,),
        in_specs=[pl.BlockSpec((tm, d), lambda i: (i, 0)), pl.BlockSpec((tm, V7X_LANES), lambda i: (i, 0))]
                 + [o_spec(k) for k in range(TOP_K)],
        out_specs=pl.BlockSpec((tm, d), lambda i: (i, 0)),
        out_shape=jax.ShapeDtypeStruct((n, d), F32),
        compiler_params=_cparams(("parallel",)),
        name="combine",
    )(x1, gate_col, o_rows, o_rows, o_rows, o_rows)


def _prep_weights(p):
    cw = p["w_dw"].shape[-1]
    o_gate = 2 * cw + 3 * MOBA_WIDTH + MEM_WIDTH
    row = lambda a: a.reshape(1, -1).astype(F32)
    tile = lambda g, n: jnp.tile(g.astype(F32), n).reshape(1, -1)
    prep = {}
    prep["proj"] = (
        row(p["g_norm1"]), p["w_in"][:, :o_gate].astype(BF16),
        _segment_ones(MOBA_WIDTH, MOBA_HEAD_DIM), _segment_ones(MEM_WIDTH, MEM_HEAD_DIM),
        tile(p["g_q_moba"], MOBA_HEADS), tile(p["g_k_moba"], MOBA_HEADS), tile(p["g_q_mem"], MEM_HEADS),
        p["w_dw"].astype(F32), row(p["b_dw"]), row(p["g_conv_ln"]), row(p["b_conv_ln"]),
    )
    prep["mem_kv"] = (row(p["g_mem_norm"]), p["w_mem_kv"].astype(BF16), _segment_ones(MEM_WIDTH, MEM_HEAD_DIM),
                      tile(p["g_k_mem"], MEM_HEADS))
    wr_t = p["w_router"].astype(F32).T
    wr_hi = wr_t.astype(BF16)
    wr_lo = (wr_t - wr_hi.astype(F32)).astype(BF16)
    prep["merge"] = (
        row(p["g_norm1"]), p["w_in"][:, o_gate:].astype(BF16), p["w_conv_out"].astype(BF16), row(p["b_conv_out"]),
        p["w_moba_o"].astype(BF16), p["w_mem_o"].astype(BF16), p["w_out"].astype(BF16), row(p["g_norm2"]),
        wr_hi, wr_lo, p["b_router"].astype(F32).reshape(-1, 1),
    )
    prep["moe"] = (p["w_gu"].astype(F32), p["b_gu"].astype(F32)[:, None, :], p["w_down"].astype(F32),
                   p["b_down"].astype(F32)[:, None, :])
    return prep


def _ffn(x2d, c, om, omem, tm, prep):
    n = x2d.shape[0]
    x1, h_rows, top_e, gate_col = _merge(x2d, c, om, omem, tm, prep["merge"])
    block_e, meta, n_rows = _route(top_e, n)
    o_rows = _moe(block_e, meta, h_rows, n_rows, *prep["moe"])
    return _combine(x1, gate_col, o_rows, tm)


_PARAM_NAMES = ("g_norm1", "w_in", "w_dw", "b_dw", "g_conv_ln", "b_conv_ln", "w_conv_out", "b_conv_out", "g_q_moba",
                "g_k_moba", "w_moba_o", "g_mem_norm", "w_mem_kv", "g_q_mem", "g_k_mem", "w_mem_o", "w_out", "g_norm2",
                "w_router", "b_router", "w_gu", "b_gu", "w_down", "b_down")

PROMPT_ROWS = 512
SAMPLE_BATCH_TILE = 64
SAMPLE_MEM_TILE = 8


def kernel(x_prompt, x_sample, mem_prompt, cache_moba_k, cache_moba_v, cache_mem_k, cache_mem_v, state_conv, page_table, g_norm1, w_in, w_dw, b_dw, g_conv_ln, b_conv_ln, w_conv_out, b_conv_out, g_q_moba, g_k_moba, w_moba_o, g_mem_norm, w_mem_kv, g_q_mem, g_k_mem, w_mem_o, w_out, g_norm2, w_router, b_router, w_gu, b_gu, w_down, b_down):
    params = dict(zip(_PARAM_NAMES, (g_norm1, w_in, w_dw, b_dw, g_conv_ln, b_conv_ln, w_conv_out, b_conv_out,
                                     g_q_moba, g_k_moba, w_moba_o, g_mem_norm, w_mem_kv, g_q_mem, g_k_mem, w_mem_o,
                                     w_out, g_norm2, w_router, b_router, w_gu, b_gu, w_down, b_down)))
    depth = g_norm1.shape[0]
    bp, lp, d = x_prompt.shape
    bs, ls, _ = x_sample.shape
    n_mem = mem_prompt.shape[1]
    y_p, y_s = x_prompt, x_sample
    outs = [[] for _ in range(8)]
    for layer in range(depth):
        prep = _prep_weights({k: v[layer] for k, v in params.items()})
        k, v, q, kb, vb, qm, c, tail = _proj(y_p, None, 1, PROMPT_ROWS, *prep["proj"])
        mk, mv = _mem_kv(mem_prompt.reshape(bp * n_mem, d), PROMPT_ROWS, *prep["mem_kv"])
        o_moba = _moba_prompt(q, kb, vb)
        o_mem = _mem_attend(qm, mk.reshape(1, bp, n_mem, MEM_WIDTH), mv.reshape(1, bp, n_mem, MEM_WIDTH), 0, 1,
                            PROMPT_ROWS)
        flat = lambda a: a.reshape(bp * lp, a.shape[-1])
        y_p = _ffn(flat(y_p), flat(c), flat(o_moba), flat(o_mem), PROMPT_ROWS, prep).reshape(bp, lp, d)
        mem_shape = (bp, n_mem, MEM_HEADS, MEM_HEAD_DIM)
        for lst, val in zip(outs[:5], (k, v, mk.reshape(mem_shape), mv.reshape(mem_shape), tail)):
            lst.append(val)
        k, v, q, kb, vb, qm, c, tail = _proj(y_s, state_conv[layer], SAMPLE_BATCH_TILE, ls, *prep["proj"])
        o_moba = _moba_sample(q, kb, vb, cache_moba_k, cache_moba_v, page_table, layer)
        o_mem = _mem_attend(qm, cache_mem_k, cache_mem_v, layer, SAMPLE_MEM_TILE, ls)
        flat = lambda a: a.reshape(bs * ls, a.shape[-1])
        y_s = _ffn(flat(y_s), flat(c), flat(o_moba), flat(o_mem), PROMPT_ROWS, prep).reshape(bs, ls, d)
        for lst, val in zip(outs[5:], (k, v, tail)):
            lst.append(val)
    kp, vp, mkp, mvp, cp, ks, vs, cs = (jnp.stack(o) for o in outs)
    return (y_p, y_s, kp, vp, mkp, mvp, cp, ks, vs, cs)
```

```python
import functools

import jax
import jax.numpy as jnp
from jax import lax
from jax.experimental import pallas as pl
from jax.experimental.pallas import tpu as pltpu

F32 = jnp.float32
BF16 = jnp.bfloat16

EPS = 1e-6
MOBA_HEADS = 8
MOBA_HEAD_DIM = 64
MOBA_WIDTH = MOBA_HEADS * MOBA_HEAD_DIM
MOBA_BLOCK = 256
MOBA_TOPK = 3
MEM_HEADS = 4
MEM_HEAD_DIM = 128
MEM_WIDTH = MEM_HEADS * MEM_HEAD_DIM
CONV_KERNEL = 31
CONV_HIST = CONV_KERNEL - 1
N_EXPERTS = 32
TOP_K = 4
SWIGLU_LIMIT = 7.0
SWIGLU_ALPHA = 1.702
MOE_BLOCK = 256

V7X_LANES = 128
V7X_SUBLANES = 8
HIST_PAD = 32
VMEM_LIMIT = 56 * 1024 * 1024


def _cparams(sem):
    return pltpu.CompilerParams(dimension_semantics=sem, vmem_limit_bytes=VMEM_LIMIT)


def _full(shape):
    n = len(shape)
    return pl.BlockSpec(shape, lambda *_: (0,) * n)


def _segment_ones(width, seg):
    r = jnp.arange(width) // seg
    return (r[:, None] == r[None, :]).astype(BF16)


def _rms(x, g):
    return x * lax.rsqrt(jnp.mean(x * x, axis=-1, keepdims=True) + EPS) * g


def _seg_rms(z, seg_ones, seg, g):
    sq = (z * z).astype(BF16)
    parts = []
    for c in range(0, z.shape[-1], 2 * V7X_LANES):
        parts.append(jnp.dot(sq[:, c:c + 2 * V7X_LANES], seg_ones[c:c + 2 * V7X_LANES, c:c + 2 * V7X_LANES],
                             preferred_element_type=F32))
    ms = jnp.concatenate(parts, axis=-1) * (1.0 / seg)
    return z * lax.rsqrt(ms + EPS) * g


def _proj_kernel(has_hist, carry, x_ref, *refs):
    if has_hist:
        hist_ref, refs = refs[0], refs[1:]
    (g1_ref, w1_ref, seg64_ref, seg128_ref, gq_ref, gk_ref, gqm_ref, wdw_ref, bdw_ref, gln_ref, bln_ref,
     k_out, v_out, q_out, kb_out, vb_out, qm_out, c_out, tail_out, xp_scr) = refs
    bt, tl, d = x_ref.shape
    rows = bt * tl
    cw = c_out.shape[-1]
    t = pl.program_id(1)

    x = x_ref[...].reshape(rows, d)
    h = _rms(x, g1_ref[...]).astype(BF16)
    z = jnp.dot(h, w1_ref[...], preferred_element_type=F32)
    o_q = 2 * cw
    o_k = o_q + MOBA_WIDTH
    o_v = o_k + MOBA_WIDTH
    o_qm = o_v + MOBA_WIDTH
    u = z[:, :cw] * jax.nn.sigmoid(z[:, cw:o_q])
    seg64 = seg64_ref[...]
    qn = _seg_rms(z[:, o_q:o_k], seg64, MOBA_HEAD_DIM, gq_ref[...])
    kn = _seg_rms(z[:, o_k:o_v], seg64, MOBA_HEAD_DIM, gk_ref[...])
    vv = z[:, o_v:o_qm]
    qmn = _seg_rms(z[:, o_qm:o_qm + MEM_WIDTH], seg128_ref[...], MEM_HEAD_DIM, gqm_ref[...])

    q_out[...] = qn.astype(BF16).reshape(bt, tl, MOBA_WIDTH)
    kb_out[...] = kn.astype(BF16).reshape(bt, tl, MOBA_WIDTH)
    vb_out[...] = vv.astype(BF16).reshape(bt, tl, MOBA_WIDTH)
    qm_out[...] = qmn.astype(BF16).reshape(bt, tl, MEM_WIDTH)
    kn3 = kn.reshape(bt, tl, MOBA_WIDTH)
    vv3 = vv.reshape(bt, tl, MOBA_WIDTH)
    for hh in range(MOBA_HEADS):
        sl = slice(hh * MOBA_HEAD_DIM, (hh + 1) * MOBA_HEAD_DIM)
        k_out[:, hh, :, :] = kn3[:, :, sl]
        v_out[:, hh, :, :] = vv3[:, :, sl]

    lo = HIST_PAD - CONV_HIST

    @pl.when(t == 0)
    def _():
        xp_scr[:, 0:HIST_PAD, :] = jnp.zeros((bt, HIST_PAD, cw), F32)
        if has_hist:
            xp_scr[:, lo:HIST_PAD, :] = hist_ref[...]

    xp_scr[:, HIST_PAD:HIST_PAD + tl, :] = u.reshape(bt, tl, cw)
    rc = min(tl, 64)
    bdw = bdw_ref[...]
    gln = gln_ref[...]
    bln = bln_ref[...]
    for r0 in range(0, tl, rc):
        window = xp_scr[:, r0:r0 + rc + HIST_PAD, :]
        acc = jnp.zeros((bt, rc, cw), F32) + bdw
        for s in range(V7X_SUBLANES):
            ext = rc if s == 0 else rc + V7X_SUBLANES
            part = None
            for q in range(HIST_PAD // V7X_SUBLANES + 1):
                j = V7X_SUBLANES * q + s - lo
                if 0 <= j < CONV_KERNEL:
                    term = wdw_ref[j:j + 1, :] * window[:, V7X_SUBLANES * q:V7X_SUBLANES * q + ext, :]
                    part = term if part is None else part + term
            acc = acc + (part if s == 0 else part[:, s:s + rc, :])
        mu = jnp.mean(acc, axis=-1, keepdims=True)
        dev = acc - mu
        var = jnp.mean(dev * dev, axis=-1, keepdims=True)
        y = dev * lax.rsqrt(var + EPS) * gln + bln
        c_out[:, r0:r0 + rc, :] = (y * jax.nn.sigmoid(y)).astype(BF16)
    tail_out[...] = xp_scr[:, tl + lo:tl + HIST_PAD, :]
    if carry:
        xp_scr[:, 0:HIST_PAD, :] = xp_scr[:, tl:tl + HIST_PAD, :]


def _proj(x, hist, bt, tl, g1, w1, seg64, seg128, gq, gk, gqm, wdw, bdw, gln, bln):
    b, l, d = x.shape
    cw = wdw.shape[-1]
    n_t = l // tl
    has_hist = hist is not None
    carry = n_t > 1
    assert b % bt == 0 and l % tl == 0 and (not carry or (tl >= HIST_PAD and bt == 1))
    tok = lambda w: pl.BlockSpec((bt, tl, w), lambda i, j: (i, j, 0))
    head = pl.BlockSpec((bt, MOBA_HEADS, tl, MOBA_HEAD_DIM), lambda i, j: (i, 0, j, 0))
    in_specs = [tok(d)]
    args = [x]
    if has_hist:
        in_specs.append(pl.BlockSpec((bt, CONV_HIST, cw), lambda i, j: (i, 0, 0)))
        args.append(hist)
    consts = [g1, w1, seg64, seg128, gq, gk, gqm, wdw, bdw, gln, bln]
    in_specs += [_full(c.shape) for c in consts]
    out_shape = [
        jax.ShapeDtypeStruct((b, MOBA_HEADS, l, MOBA_HEAD_DIM), F32),
        jax.ShapeDtypeStruct((b, MOBA_HEADS, l, MOBA_HEAD_DIM), F32),
        jax.ShapeDtypeStruct((b, l, MOBA_WIDTH), BF16),
        jax.ShapeDtypeStruct((b, l, MOBA_WIDTH), BF16),
        jax.ShapeDtypeStruct((b, l, MOBA_WIDTH), BF16),
        jax.ShapeDtypeStruct((b, l, MEM_WIDTH), BF16),
        jax.ShapeDtypeStruct((b, l, cw), BF16),
        jax.ShapeDtypeStruct((b, CONV_HIST, cw), F32),
    ]
    out_specs = [head, head, tok(MOBA_WIDTH), tok(MOBA_WIDTH), tok(MOBA_WIDTH), tok(MEM_WIDTH), tok(cw),
                 pl.BlockSpec((bt, CONV_HIST, cw), lambda i, j: (i, 0, 0))]
    return pl.pallas_call(
        functools.partial(_proj_kernel, has_hist, carry),
        grid=(b // bt, n_t),
        in_specs=in_specs,
        out_specs=out_specs,
        out_shape=out_shape,
        scratch_shapes=[pltpu.VMEM((bt, HIST_PAD + tl, cw), F32)],
        compiler_params=_cparams(("parallel", "arbitrary")),
        name="proj",
    )(*args, *consts)


def _memkv_kernel(x_ref, g_ref, w_ref, seg128_ref, gk_ref, k_out, v_out):
    h = _rms(x_ref[...], g_ref[...]).astype(BF16)
    kv = jnp.dot(h, w_ref[...], preferred_element_type=F32)
    k_out[...] = _seg_rms(kv[:, :MEM_WIDTH], seg128_ref[...], MEM_HEAD_DIM, gk_ref[...])
    v_out[...] = kv[:, MEM_WIDTH:]


def _mem_kv(mem2d, tm, g, w, seg128, gk):
    n, d = mem2d.shape
    assert n % tm == 0
    spec = pl.BlockSpec((tm, MEM_WIDTH), lambda i: (i, 0))
    return pl.pallas_call(
        _memkv_kernel,
        grid=(n // tm,),
        in_specs=[pl.BlockSpec((tm, d), lambda i: (i, 0))] + [_full(c.shape) for c in (g, w, seg128, gk)],
        out_specs=[spec, spec],
        out_shape=[jax.ShapeDtypeStruct((n, MEM_WIDTH), F32)] * 2,
        compiler_params=_cparams(("parallel",)),
        name="mem_kv",
    )(mem2d, g, w, seg128, gk)


def _mematt_kernel(q_ref, k_ref, v_ref, o_ref):
    scale = MEM_HEAD_DIM ** -0.5
    for hh in range(MEM_HEADS):
        sl = slice(hh * MEM_HEAD_DIM, (hh + 1) * MEM_HEAD_DIM)
        q = q_ref[:, :, sl]
        if len(k_ref.shape) == 5:
            k = k_ref[0, :, :, hh, :].astype(BF16)
            v = v_ref[0, :, :, hh, :].astype(BF16)
        else:
            k = k_ref[0, :, :, sl].astype(BF16)
            v = v_ref[0, :, :, sl].astype(BF16)
        s = jnp.einsum("bqd,bkd->bqk", q, k, preferred_element_type=F32) * scale
        m = jnp.max(s, axis=-1, keepdims=True)
        p = jnp.exp(s - m)
        l = jnp.sum(p, axis=-1, keepdims=True)
        o = jnp.einsum("bqk,bkd->bqd", p.astype(BF16), v, preferred_element_type=F32)
        o_ref[:, :, sl] = (o / l).astype(BF16)


def _mem_attend(qm, mk, mv, layer, bt, tq):
    b, l, w = qm.shape
    m = mk.shape[2]
    assert b % bt == 0 and l % tq == 0 and mk.shape[3:] in ((MEM_HEADS, MEM_HEAD_DIM), (MEM_WIDTH,))
    tail = mk.shape[3:]
    kv_spec = pl.BlockSpec((1, bt, m) + tail, lambda i, j: (layer, i, 0) + (0,) * len(tail))
    return pl.pallas_call(
        _mematt_kernel,
        grid=(b // bt, l // tq),
        in_specs=[pl.BlockSpec((bt, tq, w), lambda i, j: (i, j, 0)), kv_spec, kv_spec],
        out_specs=pl.BlockSpec((bt, tq, w), lambda i, j: (i, j, 0)),
        out_shape=jax.ShapeDtypeStruct((b, l, w), BF16),
        compiler_params=_cparams(("parallel", "arbitrary")),
        name="mem_attend",
    )(qm, mk, mv)


def _moba_prompt_kernel(q_ref, k_ref, v_ref, o_ref, vt_scr, km_scr, sel_scr, acc_scr, m_scr, l_scr, wq_scr):
    blk = MOBA_BLOCK
    l = k_ref.shape[1]
    nb = l // blk
    qb = pl.program_id(1)
    hd = MOBA_HEAD_DIM

    @pl.when(qb == 0)
    def _():
        lane_head = lax.broadcasted_iota(jnp.int32, (MOBA_HEADS, MOBA_WIDTH), 1) // hd
        row_head = lax.broadcasted_iota(jnp.int32, (MOBA_HEADS, MOBA_WIDTH), 0)
        head_mask = (lane_head == row_head).astype(F32)
        for n in range(nb):
            vt_scr[n] = v_ref[0, n * blk:(n + 1) * blk, :].T
            kmean = jnp.mean(k_ref[0, n * blk:(n + 1) * blk, :].astype(F32), axis=0, keepdims=True)
            km_scr[n * MOBA_HEADS:(n + 1) * MOBA_HEADS, :] = (kmean * head_mask).astype(BF16)

    qt = q_ref[0].T
    gate = jnp.dot(km_scr[...], qt, preferred_element_type=F32).reshape(nb, MOBA_HEADS, blk)
    for n in range(nb):
        rank = jnp.zeros((MOBA_HEADS, blk), jnp.int32)
        for n2 in range(nb):
            if n2 == n:
                continue
            beats = (gate[n2] >= gate[n]) if n2 < n else (gate[n2] > gate[n])
            rank = rank + jnp.where(jnp.logical_and(beats, n2 < qb), 1, 0)
        sel_scr[n] = jnp.where(jnp.logical_and(rank < MOBA_TOPK, n < qb), 1.0, 0.0)

    pairs = MOBA_HEADS // 2
    row_in_pair = lax.broadcasted_iota(jnp.int32, (2 * hd, blk), 0) // hd
    for pr in range(pairs):
        qp = qt[pr * 2 * hd:(pr + 1) * 2 * hd, :] * jnp.asarray(hd ** -0.5, BF16)
        zero = jnp.zeros_like(qp)
        wq_scr[pr] = jnp.concatenate([jnp.where(row_in_pair == 0, qp, zero), jnp.where(row_in_pair == 1, qp, zero)],
                                     axis=1)

    def attend(kblk, n, keep, first):
        s_pairs = [jnp.dot(kblk[:, pr * 2 * hd:(pr + 1) * 2 * hd], wq_scr[pr], preferred_element_type=F32)
                   for pr in range(pairs)]
        m_old = None if first else m_scr[...]
        l_old = None if first else l_scr[...]
        m_rows, l_rows, alphas, probs = [], [], [], []
        for hh in range(MOBA_HEADS):
            s = jnp.where(keep(hh), s_pairs[hh // 2][:, (hh % 2) * blk:(hh % 2 + 1) * blk], -jnp.inf)
            m_new = jnp.max(s, axis=0, keepdims=True)
            if not first:
                m_new = jnp.maximum(m_old[hh:hh + 1, :], m_new)
                alphas.append(jnp.exp(m_old[hh:hh + 1, :] - m_new))
            p = jnp.exp(s - m_new)
            p_sum = jnp.sum(p, axis=0, keepdims=True)
            l_rows.append(p_sum if first else alphas[hh] * l_old[hh:hh + 1, :] + p_sum)
            m_rows.append(m_new)
            probs.append(p.astype(BF16))
        m_scr[...] = jnp.concatenate(m_rows, axis=0)
        l_scr[...] = jnp.concatenate(l_rows, axis=0)
        pvs = [jnp.dot(vt_scr[n, hh * hd:(hh + 1) * hd, :], probs[hh], preferred_element_type=F32)
               for hh in range(MOBA_HEADS)]
        for hh in range(MOBA_HEADS):
            rows = slice(hh * hd, (hh + 1) * hd)
            acc_scr[rows, :] = pvs[hh] if first else alphas[hh] * acc_scr[rows, :] + pvs[hh]

    causal = lax.broadcasted_iota(jnp.int32, (blk, blk), 0) <= lax.broadcasted_iota(jnp.int32, (blk, blk), 1)
    attend(k_ref[0, pl.ds(pl.multiple_of(qb * blk, blk), blk), :], qb, lambda hh: causal, True)

    def past_block(n, carry):
        sel = sel_scr[n]
        attend(k_ref[0, pl.ds(pl.multiple_of(n * blk, blk), blk), :], n, lambda hh: sel[hh:hh + 1, :] > 0.0, False)
        return carry

    lax.fori_loop(0, qb, past_block, 0)

    for hh in range(MOBA_HEADS):
        acc_scr[hh * hd:(hh + 1) * hd, :] = acc_scr[hh * hd:(hh + 1) * hd, :] / l_scr[hh:hh + 1, :]
    o_ref[0] = acc_scr[...].T.astype(BF16)


def _moba_prompt(q, kb, vb):
    b, l, w = q.shape
    blk = MOBA_BLOCK
    assert l % blk == 0
    nb = l // blk
    full = pl.BlockSpec((1, l, w), lambda i, j: (i, 0, 0))
    tile = pl.BlockSpec((1, blk, w), lambda i, j: (i, j, 0))
    return pl.pallas_call(
        _moba_prompt_kernel,
        grid=(b, nb),
        in_specs=[tile, full, full],
        out_specs=tile,
        out_shape=jax.ShapeDtypeStruct((b, l, w), BF16),
        scratch_shapes=[pltpu.VMEM((nb, w, blk), BF16), pltpu.VMEM((nb * MOBA_HEADS, w), BF16),
                        pltpu.VMEM((nb, MOBA_HEADS, blk), F32), pltpu.VMEM((w, blk), F32),
                        pltpu.VMEM((MOBA_HEADS, blk), F32), pltpu.VMEM((MOBA_HEADS, blk), F32),
                        pltpu.VMEM((MOBA_HEADS // 2, 2 * MOBA_HEAD_DIM, 2 * blk), BF16)],
        compiler_params=_cparams(("parallel", "arbitrary")),
        name="moba_prompt",
    )(q, kb, vb)


SAMPLE_GROUP = 4
SAMPLE_RING = 4


def _moba_sample_kernel(layer, pt_ref, q_ref, kn_ref, vn_ref, ck_hbm, cv_hbm, o_ref,
                        kbuf, vbuf, ksem, vsem, m_scr, l_scr, g_scr, o_scr):
    b = pl.program_id(0)
    nb = pl.num_programs(0)
    n_pages = pt_ref.shape[1]
    page = kbuf.shape[4]
    ppb = MOBA_BLOCK // page
    n_full = n_pages // ppb
    grp = SAMPLE_GROUP
    n_grp = n_full // grp
    hd = MOBA_HEAD_DIM
    t = q_ref.shape[1]
    ncol = MOBA_HEADS * t
    scale = hd ** -0.5
    nt = (((1,), (1,)), ((), ()))

    def copies(bb, g, par):
        out = []
        for j in range(grp):
            for pg in range(ppb):
                pid = pt_ref[bb, (g * grp + j) * ppb + pg]
                out.append(pltpu.make_async_copy(ck_hbm.at[layer, pid], kbuf.at[par, j * ppb + pg], ksem.at[par]))
                out.append(pltpu.make_async_copy(cv_hbm.at[layer, pid], vbuf.at[par, j * ppb + pg], vsem.at[par]))
        return out

    ahead = SAMPLE_RING - 1

    @pl.when(b == 0)
    def _():
        for f in range(min(ahead, pt_ref.shape[0] * n_grp)):
            for c in copies(f // n_grp, f % n_grp, f % SAMPLE_RING):
                c.start()

    q = q_ref[0]
    row_head = lax.broadcasted_iota(jnp.int32, (V7X_LANES, MOBA_WIDTH), 0) // t
    lane_head = lax.broadcasted_iota(jnp.int32, (V7X_LANES, MOBA_WIDTH), 1) // hd
    q_tiled = jnp.concatenate([q.astype(F32)] * (V7X_LANES // t), axis=0)
    qrows = jnp.where(row_head == lane_head, q_tiled, 0.0).astype(BF16)

    def packed(buf, par, j):
        return jnp.concatenate([buf[par, j * ppb + pg].reshape(MOBA_WIDTH, page) for pg in range(ppb)],
                               axis=-1).astype(BF16)

    lane_id = lax.broadcasted_iota(jnp.int32, (V7X_LANES, V7X_LANES), 1)
    g_scr[...] = jnp.full((V7X_LANES, V7X_LANES), -jnp.inf, F32)
    m_scr[...] = jnp.full((V7X_LANES, V7X_LANES), -jnp.inf, F32)
    l_scr[...] = jnp.zeros((V7X_LANES, V7X_LANES), F32)

    def group(g, carry):
        flat = b * n_grp + g
        par = flat % SAMPLE_RING
        nxt = flat + ahead

        @pl.when(nxt < nb * n_grp)
        def _():
            for c in copies(nxt // n_grp, nxt % n_grp, nxt % SAMPLE_RING):
                c.start()

        for c in copies(b, g, par):
            c.wait()
        for j in range(grp):
            n = g * grp + j
            s_raw = jnp.dot(qrows, packed(kbuf, par, j), preferred_element_type=F32)
            s = s_raw * scale
            m = jnp.max(s, axis=-1, keepdims=True)
            p = jnp.exp(s - m)
            hit = lane_id == n
            g_scr[...] = jnp.where(hit, jnp.sum(s_raw, axis=-1, keepdims=True) * (1.0 / MOBA_BLOCK), g_scr[...])
            m_scr[...] = jnp.where(hit, m, m_scr[...])
            l_scr[...] = jnp.where(hit, jnp.sum(p, axis=-1, keepdims=True), l_scr[...])
            o_scr[n] = lax.dot_general(p[:ncol, :].astype(BF16), packed(vbuf, par, j), nt, preferred_element_type=F32)
        return carry

    lax.fori_loop(0, n_grp, group, 0)

    gate = g_scr[...]
    sel = jnp.zeros(gate.shape, jnp.bool_)
    for _ in range(min(MOBA_TOPK, n_full)):
        mx = jnp.max(gate, axis=-1, keepdims=True)
        first = jnp.min(jnp.where(gate == mx, lane_id, V7X_LANES), axis=-1, keepdims=True)
        hit = lane_id == first
        sel = jnp.logical_or(sel, hit)
        gate = jnp.where(hit, -jnp.inf, gate)

    s_new = lax.dot_general(qrows, kn_ref[0], nt, preferred_element_type=F32) * scale
    row_tok = lax.broadcasted_iota(jnp.int32, s_new.shape, 0) % t
    key_tok = lax.broadcasted_iota(jnp.int32, s_new.shape, 1)
    s_new = jnp.where(key_tok <= row_tok, s_new, -jnp.inf)
    m_blk = jnp.where(sel, m_scr[...], -jnp.inf)
    m_fin = jnp.maximum(jnp.max(m_blk, axis=-1, keepdims=True), jnp.max(s_new, axis=-1, keepdims=True))
    w_blk = jnp.where(sel, jnp.exp(m_blk - m_fin), 0.0)
    p_new = jnp.exp(s_new - m_fin)
    l_fin = jnp.sum(w_blk * l_scr[...], axis=-1, keepdims=True) + jnp.sum(p_new, axis=-1, keepdims=True)
    o_fin = jnp.dot(p_new[:ncol, :].astype(BF16), vn_ref[0], preferred_element_type=F32)
    for n in range(n_full):
        o_fin = o_fin + w_blk[:ncol, n:n + 1] * o_scr[n]
    o_fin = o_fin / l_fin[:ncol, :]
    for hh in range(MOBA_HEADS):
        o_ref[0, :, hh * hd:(hh + 1) * hd] = o_fin[hh * t:(hh + 1) * t, hh * hd:(hh + 1) * hd].astype(BF16)


def _moba_sample(q, kn, vn, cache_k, cache_v, page_table, layer):
    b, t, w = q.shape
    n_pages = page_table.shape[1]
    _, _, heads, page, hd = cache_k.shape
    cache_k, cache_v = jnp.swapaxes(cache_k, 3, 4), jnp.swapaxes(cache_v, 3, 4)
    ppb = MOBA_BLOCK // page
    assert MOBA_BLOCK % page == 0 and n_pages % ppb == 0 and heads == MOBA_HEADS and hd == MOBA_HEAD_DIM
    n_full = n_pages // ppb
    assert n_full % SAMPLE_GROUP == 0 and MOBA_TOPK <= n_full <= V7X_LANES and heads * t <= V7X_LANES
    tok = pl.BlockSpec((1, t, w), lambda i, pt: (i, 0, 0))
    any_spec = pl.BlockSpec(memory_space=pl.ANY)
    pages = SAMPLE_GROUP * ppb
    grid_spec = pltpu.PrefetchScalarGridSpec(
        num_scalar_prefetch=1,
        grid=(b,),
        in_specs=[tok, tok, tok, any_spec, any_spec],
        out_specs=tok,
        scratch_shapes=[pltpu.VMEM((SAMPLE_RING, pages, heads, hd, page), F32),
                        pltpu.VMEM((SAMPLE_RING, pages, heads, hd, page), F32),
                        pltpu.SemaphoreType.DMA((SAMPLE_RING,)), pltpu.SemaphoreType.DMA((SAMPLE_RING,)),
                        pltpu.VMEM((V7X_LANES, V7X_LANES), F32), pltpu.VMEM((V7X_LANES, V7X_LANES), F32),
                        pltpu.VMEM((V7X_LANES, V7X_LANES), F32), pltpu.VMEM((n_full, heads * t, w), F32)],
    )
    return pl.pallas_call(
        functools.partial(_moba_sample_kernel, layer),
        grid_spec=grid_spec,
        out_shape=jax.ShapeDtypeStruct((b, t, w), BF16),
        compiler_params=_cparams(("arbitrary",)),
        name="moba_sample",
    )(page_table, q, kn, vn, cache_k, cache_v)


def _merge_kernel(x_ref, c_ref, om_ref, omem_ref, g1_ref, wg_ref, wco_ref, bco_ref, wmo_ref, wmemo_ref, wout_ref,
                  g2_ref, wr_hi_ref, wr_lo_ref, br_ref, x1_out, h2_out, e_out, gate_out):
    tm, d = x_ref.shape
    x = x_ref[...]
    h = _rms(x, g1_ref[...]).astype(BF16)
    gates = jax.nn.sigmoid(jnp.dot(h, wg_ref[...], preferred_element_type=F32))
    u_conv = jnp.dot(c_ref[...], wco_ref[...], preferred_element_type=F32) + bco_ref[...]
    u_moba = jnp.dot(om_ref[...], wmo_ref[...], preferred_element_type=F32)
    u_mem = jnp.dot(omem_ref[...], wmemo_ref[...], preferred_element_type=F32)
    merged = gates[:, :d] * u_conv + gates[:, d:2 * d] * u_moba + gates[:, 2 * d:] * u_mem
    x1 = x + jnp.dot(merged.astype(BF16), wout_ref[...], preferred_element_type=F32)
    x1_out[...] = x1
    h2 = _rms(x1, g2_ref[...])
    for c in range(d // V7X_LANES):
        h2_out[pl.ds(c, tm, stride=d // V7X_LANES), :] = h2[:, c * V7X_LANES:(c + 1) * V7X_LANES]

    h2_hi = h2.astype(BF16)
    h2_lo = (h2 - h2_hi.astype(F32)).astype(BF16)
    nt = (((1,), (1,)), ((), ()))
    logits = (lax.dot_general(wr_hi_ref[...], h2_hi, nt, preferred_element_type=F32)
              + lax.dot_general(wr_hi_ref[...], h2_lo, nt, preferred_element_type=F32)
              + lax.dot_general(wr_lo_ref[...], h2_hi, nt, preferred_element_type=F32)) + br_ref[...]
    e_id = lax.broadcasted_iota(jnp.int32, logits.shape, 0)
    tops, ids = [], []
    for _ in range(TOP_K):
        mx = jnp.max(logits, axis=0, keepdims=True)
        first = jnp.min(jnp.where(logits == mx, e_id, N_EXPERTS), axis=0, keepdims=True)
        tops.append(mx)
        ids.append(first)
        logits = jnp.where(e_id == first, -jnp.inf, logits)
    ex = [jnp.exp(v - tops[0]) for v in tops]
    denom = ex[0] + ex[1] + ex[2] + ex[3]
    e_out[...] = jnp.concatenate(ids, axis=0)
    gate_rows = jnp.concatenate([v / denom for v in ex] + [jnp.zeros((V7X_LANES - TOP_K, tm), F32)], axis=0)
    gate_out[...] = gate_rows.T


def _merge(x2d, c, om, omem, tm, consts):
    n, d = x2d.shape
    assert n % tm == 0
    row = lambda w: pl.BlockSpec((tm, w), lambda i: (i, 0))
    n_chunk = d // V7X_LANES
    return pl.pallas_call(
        _merge_kernel,
        grid=(n // tm,),
        in_specs=[row(d), row(c.shape[1]), row(om.shape[1]), row(omem.shape[1])] + [_full(a.shape) for a in consts],
        out_specs=[row(d), pl.BlockSpec((tm * n_chunk, V7X_LANES), lambda i: (i, 0)),
                   pl.BlockSpec((TOP_K, tm), lambda i: (0, i)), row(V7X_LANES)],
        out_shape=[jax.ShapeDtypeStruct((n, d), F32), jax.ShapeDtypeStruct((n * n_chunk, V7X_LANES), F32),
                   jax.ShapeDtypeStruct((TOP_K, n), jnp.int32), jax.ShapeDtypeStruct((n, V7X_LANES), F32)],
        compiler_params=_cparams(("parallel",)),
        name="merge",
    )(x2d, c, om, omem, *consts)


META_ROWS = V7X_SUBLANES
TOK_ROW0 = 0
DST_ROW0 = MOE_BLOCK // V7X_LANES


def _moe_kernel(first_spill, be_ref, meta_hbm, h_hbm, wgu_ref, bgu_ref, wd_ref, bd_ref, o_hbm,
                meta_smem, msem, xbuf, xsem, ybuf, ysem, wgu_bf, wd_bf):
    i = pl.program_id(0)
    n = be_ref[pl.num_programs(0)]
    rows = MOE_BLOCK
    n_chunk = xbuf.shape[1] // rows

    def meta_copy(blk):
        return pltpu.make_async_copy(meta_hbm.at[pl.ds(pl.multiple_of(blk * META_ROWS, META_ROWS), META_ROWS)],
                                     meta_smem.at[blk % 3], msem.at[blk % 3])

    def gather_start(blk, slot):
        ms = blk % 3
        for r in range(rows):
            src = pl.multiple_of(meta_smem[ms, TOK_ROW0 + r // V7X_LANES, r % V7X_LANES], n_chunk)
            pltpu.make_async_copy(h_hbm.at[pl.ds(src, n_chunk)], xbuf.at[slot, pl.ds(r * n_chunk, n_chunk)],
                                  xsem.at[slot]).start(priority=r % 2)

    def scatter_start(blk, slot, r0, r1):
        ms = blk % 3
        for r in range(r0, r1):
            dst = pl.multiple_of(meta_smem[ms, DST_ROW0 + r // V7X_LANES, r % V7X_LANES], n_chunk)
            pltpu.make_async_copy(ybuf.at[slot, pl.ds(r * n_chunk, n_chunk)], o_hbm.at[pl.ds(dst, n_chunk)],
                                  ysem.at[slot]).start(priority=r % 2)

    def gather_wait(slot):
        pltpu.make_async_copy(h_hbm.at[pl.ds(0, rows * n_chunk)], xbuf.at[slot], xsem.at[slot]).wait()

    def scatter_wait(slot):
        pltpu.make_async_copy(ybuf.at[slot], o_hbm.at[pl.ds(0, rows * n_chunk)], ysem.at[slot]).wait()

    slot = i % 2

    @pl.when(i < n)
    def _():
        @pl.when(i == 0)
        def _():
            meta_copy(0).start()
            ybuf[0] = jnp.zeros(ybuf.shape[1:], F32)
            n_spill = o_hbm.shape[0] - first_spill * n_chunk
            fills = [pltpu.make_async_copy(ybuf.at[0, pl.ds(0, min(rows * n_chunk, n_spill - s))],
                                           o_hbm.at[pl.ds(first_spill * n_chunk + s, min(rows * n_chunk, n_spill - s))],
                                           ysem.at[0])
                     for s in range(0, n_spill, rows * n_chunk)]
            for f in fills:
                f.start()
            for f in fills:
                f.wait()
            meta_copy(0).wait()
            gather_start(0, 0)

            @pl.when(n > 1)
            def _():
                meta_copy(1).start()

        @pl.when(i + 2 < n)
        def _():
            meta_copy(i + 2).start()

        @pl.when(i + 1 < n)
        def _():
            meta_copy(i + 1).wait()

        @pl.when(i >= 2)
        def _():
            scatter_wait(slot)

        @pl.when(jnp.logical_or(i == 0, be_ref[i] != be_ref[jnp.maximum(i - 1, 0)]))
        def _():
            wgu_bf[...] = wgu_ref[0].astype(BF16)
            wd_bf[...] = wd_ref[0].astype(BF16)

        gather_start(jnp.minimum(i + 1, n - 1), 1 - slot)
        gather_wait(slot)
        x = jnp.concatenate([xbuf[slot, pl.ds(c, rows, stride=n_chunk), :] for c in range(n_chunk)],
                            axis=-1).astype(BF16)
        gu = jnp.dot(x, wgu_bf[...], preferred_element_type=F32) + bgu_ref[0]
        dff = gu.shape[-1] // 2
        g = jnp.minimum(gu[:, :dff], SWIGLU_LIMIT)
        u = jnp.clip(gu[:, dff:], -SWIGLU_LIMIT, SWIGLU_LIMIT)
        act = (g * jax.nn.sigmoid(SWIGLU_ALPHA * g) * (u + 1.0)).astype(BF16)
        y = jnp.dot(act, wd_bf[...], preferred_element_type=F32) + bd_ref[0]
        for c in range(n_chunk):
            ybuf[slot, pl.ds(c, rows, stride=n_chunk), :] = y[:, c * V7X_LANES:(c + 1) * V7X_LANES]
        scatter_start(i, slot, 0, rows)

        @pl.when(i == n - 1)
        def _():
            gather_wait(1 - slot)
            scatter_wait(slot)

            @pl.when(n > 1)
            def _():
                scatter_wait(1 - slot)


def _moe(block_e, meta, h_rows, n_out_rows, wgu, bgu, wd, bd):
    n_blocks = block_e.shape[0] - 1
    d = wgu.shape[1]
    lanes = h_rows.shape[1]
    n_chunk = d // lanes
    any_spec = pl.BlockSpec(memory_space=pl.ANY)
    grid_spec = pltpu.PrefetchScalarGridSpec(
        num_scalar_prefetch=1,
        grid=(n_blocks,),
        in_specs=[any_spec, any_spec,
                  pl.BlockSpec((1, d, wgu.shape[2]), lambda i, be: (be[i], 0, 0)),
                  pl.BlockSpec((1, 1, bgu.shape[2]), lambda i, be: (be[i], 0, 0)),
                  pl.BlockSpec((1, wd.shape[1], d), lambda i, be: (be[i], 0, 0)),
                  pl.BlockSpec((1, 1, d), lambda i, be: (be[i], 0, 0))],
        out_specs=any_spec,
        scratch_shapes=[pltpu.SMEM((3, META_ROWS, V7X_LANES), jnp.int32), pltpu.SemaphoreType.DMA((3,)),
                        pltpu.VMEM((2, MOE_BLOCK * n_chunk, lanes), F32), pltpu.SemaphoreType.DMA((2,)),
                        pltpu.VMEM((2, MOE_BLOCK * n_chunk, lanes), F32), pltpu.SemaphoreType.DMA((2,)),
                        pltpu.VMEM(wgu.shape[1:], BF16), pltpu.VMEM(wd.shape[1:], BF16)],
    )
    first_spill = TOP_K * (h_rows.shape[0] // n_chunk)
    return pl.pallas_call(
        functools.partial(_moe_kernel, first_spill),
        grid_spec=grid_spec,
        out_shape=jax.ShapeDtypeStruct((n_out_rows * n_chunk, lanes), F32),
        compiler_params=_cparams(("arbitrary",)),
        name="moe",
    )(block_e, meta * n_chunk, h_rows, wgu, bgu, wd, bd)


def _route(top_e, n_tok):
    nk = n_tok * TOP_K
    flat_e = top_e.reshape(nk)
    counts = jnp.sum((flat_e[None, :] == jnp.arange(N_EXPERTS, dtype=jnp.int32)[:, None]).astype(jnp.int32), axis=1)
    padded = (counts + MOE_BLOCK - 1) // MOE_BLOCK * MOE_BLOCK
    pad_end = jnp.cumsum(padded)
    n_blocks = (nk + N_EXPERTS * (MOE_BLOCK - 1) + MOE_BLOCK - 1) // MOE_BLOCK
    n_rows = n_blocks * MOE_BLOCK
    shift = (nk + 1).bit_length()
    assert N_EXPERTS < (1 << (31 - shift))
    pad_cum = jnp.cumsum(padded - counts)
    pad_e = jnp.sum((pad_cum[None, :] <= jnp.arange(n_rows - nk, dtype=jnp.int32)[:, None]).astype(jnp.int32), axis=1)
    packed = jnp.concatenate([(flat_e << shift) + jnp.arange(1, nk + 1, dtype=jnp.int32), pad_e << shift])
    row_asg = (lax.sort(packed, is_stable=False) & ((1 << shift) - 1)) - 1
    is_real = row_asg >= 0
    row_tok = jnp.where(is_real, row_asg % n_tok, 0)
    spill = nk + jnp.cumsum(jnp.logical_not(is_real).astype(jnp.int32)) - 1
    row_dst = jnp.where(is_real, row_asg, spill)
    blk_start = jnp.arange(n_blocks, dtype=jnp.int32) * MOE_BLOCK
    block_e = jnp.sum((pad_end[None, :] <= blk_start[:, None]).astype(jnp.int32), axis=1)
    n_used = pad_end[N_EXPERTS - 1] // MOE_BLOCK
    last_e = jnp.max(jnp.where(counts > 0, jnp.arange(N_EXPERTS, dtype=jnp.int32), 0))
    block_e = jnp.concatenate([jnp.minimum(block_e, last_e), n_used[None]]).astype(jnp.int32)
    per = MOE_BLOCK // V7X_LANES
    meta = jnp.concatenate([row_tok.reshape(n_blocks, per, V7X_LANES), row_dst.reshape(n_blocks, per, V7X_LANES),
                            jnp.zeros((n_blocks, META_ROWS - 2 * per, V7X_LANES), jnp.int32)], axis=1)
    return block_e, meta.reshape(n_blocks * META_ROWS, V7X_LANES), n_rows


def _combine_kernel(x1_ref, gate_ref, o0_ref, o1_ref, o2_ref, o3_ref, y_ref):
    gate = gate_ref[...]
    tm, d = x1_ref.shape
    n_chunk = d // V7X_LANES
    cols = [gate[:, k:k + 1] for k in range(TOP_K)]
    for c in range(n_chunk):
        sl = slice(c * V7X_LANES, (c + 1) * V7X_LANES)
        acc = x1_ref[:, sl]
        for k, o_ref in enumerate((o0_ref, o1_ref, o2_ref, o3_ref)):
            acc = acc + cols[k] * o_ref[pl.ds(c, tm, stride=n_chunk), :]
        y_ref[:, sl] = acc


def _combine(x1, gate_col, o_rows, tm):
    n, d = x1.shape
    lanes = o_rows.shape[1]
    n_chunk = d // lanes
    assert n % tm == 0
    per = n // tm
    o_spec = lambda k: pl.BlockSpec((tm * n_chunk, lanes), lambda i, k=k: (k * per + i, 0))
    return pl.pallas_call(
        _combine_kernel,
        grid=(per,),
        in_specs=[pl.BlockSpec((tm, d), lambda i: (i, 0)), pl.BlockSpec((tm, V7X_LANES), lambda i: (i, 0))]
                 + [o_spec(k) for k in range(TOP_K)],
        out_specs=pl.BlockSpec((tm, d), lambda i: (i, 0)),
        out_shape=jax.ShapeDtypeStruct((n, d), F32),
        compiler_params=_cparams(("parallel",)),
        name="combine",
    )(x1, gate_col, o_rows, o_rows, o_rows, o_rows)


def _prep_weights(p):
    cw = p["w_dw"].shape[-1]
    o_gate = 2 * cw + 3 * MOBA_WIDTH + MEM_WIDTH
    row = lambda a: a.reshape(1, -1).astype(F32)
    tile = lambda g, n: jnp.tile(g.astype(F32), n).reshape(1, -1)
    prep = {}
    prep["proj"] = (
        row(p["g_norm1"]), p["w_in"][:, :o_gate].astype(BF16),
        _segment_ones(MOBA_WIDTH, MOBA_HEAD_DIM), _segment_ones(MEM_WIDTH, MEM_HEAD_DIM),
        tile(p["g_q_moba"], MOBA_HEADS), tile(p["g_k_moba"], MOBA_HEADS), tile(p["g_q_mem"], MEM_HEADS),
        p["w_dw"].astype(F32), row(p["b_dw"]), row(p["g_conv_ln"]), row(p["b_conv_ln"]),
    )
    prep["mem_kv"] = (row(p["g_mem_norm"]), p["w_mem_kv"].astype(BF16), _segment_ones(MEM_WIDTH, MEM_HEAD_DIM),
                      tile(p["g_k_mem"], MEM_HEADS))
    wr_t = p["w_router"].astype(F32).T
    wr_hi = wr_t.astype(BF16)
    wr_lo = (wr_t - wr_hi.astype(F32)).astype(BF16)
    prep["merge"] = (
        row(p["g_norm1"]), p["w_in"][:, o_gate:].astype(BF16), p["w_conv_out"].astype(BF16), row(p["b_conv_out"]),
        p["w_moba_o"].astype(BF16), p["w_mem_o"].astype(BF16), p["w_out"].astype(BF16), row(p["g_norm2"]),
        wr_hi, wr_lo, p["b_router"].astype(F32).reshape(-1, 1),
    )
    prep["moe"] = (p["w_gu"].astype(F32), p["b_gu"].astype(F32)[:, None, :], p["w_down"].astype(F32),
                   p["b_down"].astype(F32)[:, None, :])
    return prep


def _ffn(x2d, c, om, omem, tm, prep):
    n = x2d.shape[0]
    x1, h_rows, top_e, gate_col = _merge(x2d, c, om, omem, tm, prep["merge"])
    block_e, meta, n_rows = _route(top_e, n)
    o_rows = _moe(block_e, meta, h_rows, n_rows, *prep["moe"])
    return _combine(x1, gate_col, o_rows, tm)


_PARAM_NAMES = ("g_norm1", "w_in", "w_dw", "b_dw", "g_conv_ln", "b_conv_ln", "w_conv_out", "b_conv_out", "g_q_moba",
                "g_k_moba", "w_moba_o", "g_mem_norm", "w_mem_kv", "g_q_mem", "g_k_mem", "w_mem_o", "w_out", "g_norm2",
                "w_router", "b_router", "w_gu", "b_gu", "w_down", "b_down")

PROMPT_ROWS = 512
SAMPLE_BATCH_TILE = 64
SAMPLE_MEM_TILE = 8


def kernel(x_prompt, x_sample, mem_prompt, cache_moba_k, cache_moba_v, cache_mem_k, cache_mem_v, state_conv, page_table, g_norm1, w_in, w_dw, b_dw, g_conv_ln, b_conv_ln, w_conv_out, b_conv_out, g_q_moba, g_k_moba, w_moba_o, g_mem_norm, w_mem_kv, g_q_mem, g_k_mem, w_mem_o, w_out, g_norm2, w_router, b_router, w_gu, b_gu, w_down, b_down):
    params = dict(zip(_PARAM_NAMES, (g_norm1, w_in, w_dw, b_dw, g_conv_ln, b_conv_ln, w_conv_out, b_conv_out,
                                     g_q_moba, g_k_moba, w_moba_o, g_mem_norm, w_mem_kv, g_q_mem, g_k_mem, w_mem_o,
                                     w_out, g_norm2, w_router, b_router, w_gu, b_gu, w_down, b_down)))
    depth = g_norm1.shape[0]
    bp, lp, d = x_prompt.shape
    bs, ls, _ = x_sample.shape
    n_mem = mem_prompt.shape[1]
    y_p, y_s = x_prompt, x_sample
    outs = [[] for _ in range(8)]
    for layer in range(depth):
        prep = _prep_weights({k: v[layer] for k, v in params.items()})
        k, v, q, kb, vb, qm, c, tail = _proj(y_p, None, 1, PROMPT_ROWS, *prep["proj"])
        mk, mv = _mem_kv(mem_prompt.reshape(bp * n_mem, d), PROMPT_ROWS, *prep["mem_kv"])
        o_moba = _moba_prompt(q, kb, vb)
        o_mem = _mem_attend(qm, mk.reshape(1, bp, n_mem, MEM_WIDTH), mv.reshape(1, bp, n_mem, MEM_WIDTH), 0, 1,
                            PROMPT_ROWS)
        flat = lambda a: a.reshape(bp * lp, a.shape[-1])
        y_p = _ffn(flat(y_p), flat(c), flat(o_moba), flat(o_mem), PROMPT_ROWS, prep).reshape(bp, lp, d)
        mem_shape = (bp, n_mem, MEM_HEADS, MEM_HEAD_DIM)
        for lst, val in zip(outs[:5], (k, v, mk.reshape(mem_shape), mv.reshape(mem_shape), tail)):
            lst.append(val)
        k, v, q, kb, vb, qm, c, tail = _proj(y_s, state_conv[layer], SAMPLE_BATCH_TILE, ls, *prep["proj"])
        o_moba = _moba_sample(q, kb, vb, cache_moba_k, cache_moba_v, page_table, layer)
        o_mem = _mem_attend(qm, cache_mem_k, cache_mem_v, layer, SAMPLE_MEM_TILE, ls)
        flat = lambda a: a.reshape(bs * ls, a.shape[-1])
        y_s = _ffn(flat(y_s), flat(c), flat(o_moba), flat(o_mem), PROMPT_ROWS, prep).reshape(bs, ls, d)
        for lst, val in zip(outs[5:], (k, v, tail)):
            lst.append(val)
    kp, vp, mkp, mvp, cp, ks, vs, cs = (jnp.stack(o) for o in outs)
    return (y_p, y_s, kp, vp, mkp, mvp, cp, ks, vs, cs)
```
